```python
import math
import jax, jax.numpy as jnp
from jax import lax
import numpy as np


D_MODEL = 1024
BATCH = 4
SEQ = 8192
DEPTH = 2
DEC_BATCH = 2
DEC_SEQ = 8192
PAST_LEN = 128

N_BRANCH = 4
BR_W = D_MODEL // 4
D_FF = ((8 * D_MODEL // 3 + 127) // 128) * 128
SHORT_CONV = 3
CHUNK = 64
EPS = 1e-6
HY_C = BR_W
HY_ORDER = 2
HY_BANDS = 16
HY_EMB = 1 + 2 * HY_BANDS
HY_FH = 64
HY_DECAY_MIN = 3.07
HY_DECAY_MAX = 15.35
M2_H = 4
M2_P = BR_W // M2_H
M2_N = 64
M2_G = 2
M2_W = BR_W
GLA_H = 4
GLA_DV = BR_W // GLA_H
GLA_DK = GLA_DV // 2
GLA_R = 16
GLA_GATE_NORM = 16.0
HG_H = 4
HG_DK = BR_W // HG_H
HG_DV = BR_W // HG_H

SPLIT_SIZES = (
    3 * HY_C,
    M2_W,
    M2_W + 2 * M2_G * M2_N,
    2 * M2_H,
    GLA_H * GLA_DK,
    GLA_H * GLA_DK,
    GLA_H * GLA_DV,
    2 * GLA_R,
    GLA_H * GLA_DV,
    HG_H * HG_DK,
    2 * HG_H * HG_DK,
    HG_H * HG_DV,
    HG_H * HG_DV,
    N_BRANCH * D_MODEL,
)
N_IN = sum(SPLIT_SIZES)
SPLIT_POINTS = tuple(int(s) for s in np.cumsum(SPLIT_SIZES)[:-1])

kernel_name = 'hybrid_bidir_hyena_ssd_gla_hgrn2_encoder'


def rms_norm(x, w):
    xf = x.astype(jnp.float32)
    y = xf * lax.rsqrt(jnp.mean(xf * xf, axis=-1, keepdims=True) + EPS)
    return (y * w.astype(jnp.float32)).astype(x.dtype)


def swiglu(x, w_gu, w_down):
    gate, up = jnp.split(x @ w_gu, 2, axis=-1)
    return (jax.nn.silu(gate) * up) @ w_down


def short_conv(x, w, b):
    K = w.shape[0]
    pad = K // 2
    L = x.shape[1]
    xp = jnp.pad(x, ((0, 0), (pad, pad), (0, 0)))
    out = b
    for k in range(K):
        out = out + xp[:, k:k + L] * w[k]
    return out


def _fb(fwd, bwd):
    return jnp.concatenate([fwd, jnp.flip(bwd, axis=1)], axis=2)


def _unfb(o, h):
    return o[:, :, :h] + jnp.flip(o[:, :, h:], axis=1)


def _to_chunks(t):
    b, l, h, d = t.shape
    return t.reshape(b, l // CHUNK, CHUNK, h, d).transpose(1, 0, 3, 2, 4)


def _from_chunks(t):
    nc, b, h, c, d = t.shape
    return t.transpose(1, 0, 3, 2, 4).reshape(b, nc * c, h, d)


def _masked_exp(diff, mask):
    return jnp.where(mask, jnp.exp(jnp.where(mask, diff, 0.0)), 0.0)


def chunked_gla(q, k, v, log_a):
    f32 = jnp.float32
    b, l, h, dk = q.shape
    dv = v.shape[-1]
    mask = jnp.tril(jnp.ones((CHUNK, CHUNK), dtype=bool))

    def step(S, inp):
        qc, kc, vc, gc = inp
        cb = jnp.cumsum(gc, axis=2)
        cb_last = cb[:, :, -1:]
        o = jnp.einsum('bhtd,bhde->bhte', qc * jnp.exp(cb), S)
        decay = _masked_exp(cb[:, :, :, None] - cb[:, :, None], mask[:, :, None])
        att = jnp.einsum('bhtd,bhsd,bhtsd->bhts', qc, kc, decay)
        o = o + jnp.einsum('bhts,bhse->bhte', att, vc)
        S = jnp.exp(cb_last[:, :, 0])[..., None] * S + jnp.einsum(
            'bhsd,bhse->bhde', kc * jnp.exp(cb_last - cb), vc)
        return S, o

    S0 = jnp.zeros((b, h, dk, dv), f32)
    xs = tuple(_to_chunks(t.astype(f32)) for t in (q, k, v, log_a))
    _, o = lax.scan(step, S0, xs)
    return _from_chunks(o).astype(v.dtype)


def chunked_ssd(x, dt, A, Bm, Cm):
    f32 = jnp.float32
    b, l, h, p = x.shape
    n = Bm.shape[-1]
    mask = jnp.tril(jnp.ones((CHUNK, CHUNK), dtype=bool))
    A = A.astype(f32)

    def step(S, inp):
        xc, dtc, Bc, Cc = inp
        dtc = dtc[..., 0]
        cum = jnp.cumsum(dtc * A[:, None], axis=-1)
        seg = _masked_exp(cum[..., :, None] - cum[..., None, :], mask)
        xdt = xc * dtc[..., None]
        scores = jnp.einsum('bhtn,bhsn->bhts', Cc, Bc) * seg
        y = jnp.einsum('bhts,bhsp->bhtp', scores, xdt) + jnp.einsum(
            'bhtn,bhnp->bhtp', Cc * jnp.exp(cum)[..., None], S)
        S = jnp.exp(cum[..., -1])[..., None, None] * S + jnp.einsum(
            'bhsn,bhsp->bhnp', Bc * jnp.exp(cum[..., -1:] - cum)[..., None], xdt)
        return S, y

    S0 = jnp.zeros((b, h, n, p), f32)
    xs = (_to_chunks(x.astype(f32)), _to_chunks(dt.astype(f32)[..., None]),
          _to_chunks(Bm.astype(f32)), _to_chunks(Cm.astype(f32)))
    _, y = lax.scan(step, S0, xs)
    return _from_chunks(y).astype(x.dtype)


def hyena_filter_spectra(L, w1, b1, w2, b2, freq, w3, decay):
    f32 = jnp.float32
    t = jnp.linspace(0.0, 1.0, L, dtype=f32)[:, None]
    bands = jnp.linspace(1e-4, HY_BANDS - 1, HY_BANDS, dtype=f32)
    w = (2.0 * math.pi / L) * jnp.arange(L, dtype=f32)[:, None]
    z = jnp.concatenate([t, jnp.cos(bands * w), -jnp.sin(bands * w)], axis=-1)
    freq = freq.astype(f32)
    h = jnp.sin(freq[0] * (z @ w1.astype(f32) + b1.astype(f32)))
    h = jnp.sin(freq[1] * (h @ w2.astype(f32) + b2.astype(f32)))
    h = (h @ w3.astype(f32)) * jnp.exp(-t * jnp.abs(decay.astype(f32)))
    h = h.reshape(L, HY_ORDER, 2, HY_C)
    hf, hb = h[:, :, 0], h[:, :, 1]
    g = jnp.concatenate([hf, jnp.zeros((1, HY_ORDER, HY_C), f32), jnp.flip(hb, axis=0)[:L - 1]], axis=0)
    return jnp.fft.rfft(g, axis=0)


def fft_long_conv(u, spec, bias):
    L = u.shape[1]
    uf = u.astype(jnp.float32)
    y = jnp.fft.irfft(jnp.fft.rfft(uf, n=2 * L, axis=1) * spec, n=2 * L, axis=1)[:, :L]
    return (y + uf * bias.astype(jnp.float32)).astype(u.dtype)


def mixer(h, lp):
    f32 = jnp.float32
    bsz, L, _ = h.shape
    (hy_in, m2_z, m2_xbc, m2_dt, gla_q, gla_k, gla_v, gla_a, gla_g,
     hg_q, hg_f, hg_i, hg_g, gate_in) = jnp.split(h @ lp['w_in'], SPLIT_POINTS, axis=-1)

    u = short_conv(hy_in, lp['hy_conv_w'], lp['hy_conv_b'])
    v, x1, x2 = jnp.split(u, 3, axis=-1)
    spec = hyena_filter_spectra(L, lp['hy_filt_w1'], lp['hy_filt_b1'], lp['hy_filt_w2'], lp['hy_filt_b2'],
                                lp['hy_filt_freq'], lp['hy_filt_w3'], lp['hy_decay'])
    z = x1 * fft_long_conv(v, spec[:, 0], lp['hy_bias'][0])
    y_a = x2 * fft_long_conv(z, spec[:, 1], lp['hy_bias'][1])

    xbc = jax.nn.silu(short_conv(m2_xbc, lp['m2_conv_w'], lp['m2_conv_b']))
    mx, mB, mC = jnp.split(xbc, [M2_W, M2_W + M2_G * M2_N], axis=-1)
    mx = mx.reshape(bsz, L, M2_H, M2_P)
    rep = M2_H // M2_G
    mB = jnp.repeat(mB.reshape(bsz, L, M2_G, M2_N), rep, axis=2)
    mC = jnp.repeat(mC.reshape(bsz, L, M2_G, M2_N), rep, axis=2)
    dt = jax.nn.softplus((m2_dt + lp['m2_dt_bias']).astype(f32)).reshape(bsz, L, 2, M2_H)
    A = -jnp.exp(lp['m2_A_log'].astype(f32))
    y = chunked_ssd(_fb(mx, mx), _fb(dt[:, :, 0], dt[:, :, 1]), A, _fb(mB, mB), _fb(mC, mC))
    y = _unfb(y, M2_H) + lp['m2_D'][:, None] * mx
    y_b = rms_norm(y.reshape(bsz, L, M2_W) * jax.nn.silu(m2_z), lp['m2_norm'])

    q = gla_q.reshape(bsz, L, GLA_H, GLA_DK) * GLA_DK ** -0.5
    k = gla_k.reshape(bsz, L, GLA_H, GLA_DK)
    v = gla_v.reshape(bsz, L, GLA_H, GLA_DV)
    la = jnp.einsum('blcr,crk->blck', gla_a.reshape(bsz, L, 2, GLA_R), lp['gla_w_gate2']) + lp['gla_b_gate']
    la = (jax.nn.log_sigmoid(la.astype(f32)) / GLA_GATE_NORM).reshape(bsz, L, 2, GLA_H, GLA_DK)
    o = _unfb(chunked_gla(_fb(q, q), _fb(k, k), _fb(v, v), _fb(la[:, :, 0], la[:, :, 1])), GLA_H)
    y_c = (rms_norm(o, lp['gla_norm']) * jax.nn.silu(gla_g.reshape(bsz, L, GLA_H, GLA_DV))).reshape(bsz, L, BR_W)

    lb = lp['hg_lb'].reshape(2, HG_H, HG_DK)
    zf = hg_f.reshape(bsz, L, 2, HG_H, HG_DK).astype(f32)
    log_f = jax.nn.log_sigmoid(zf) + jnp.log1p(lb * jnp.exp(-zf))
    kf = (1.0 - lb) * jax.nn.sigmoid(-zf)
    q = hg_q.reshape(bsz, L, HG_H, HG_DK) * HG_DK ** -0.5
    i = hg_i.reshape(bsz, L, HG_H, HG_DV)
    o = _unfb(chunked_gla(_fb(q, q), _fb(kf[:, :, 0], kf[:, :, 1]), _fb(i, i),
                          _fb(log_f[:, :, 0], log_f[:, :, 1])), HG_H)
    y_d = (rms_norm(o, lp['hg_norm']) * jax.nn.silu(hg_g.reshape(bsz, L, HG_H, HG_DV))).reshape(bsz, L, BR_W)

    gates = jax.nn.sigmoid(gate_in.reshape(bsz, L, N_BRANCH, D_MODEL))
    ys = (y_a, y_b, y_c, y_d)
    merged = sum(gates[:, :, n] * (ys[n] @ lp['w_branch'][n]) for n in range(N_BRANCH))
    return merged @ lp['w_out']


def encoder(x, p):
    for l in range(DEPTH):
        lp = {name: arr[l] for name, arr in p.items()}
        x = x + 0.5 * rms_norm(swiglu(rms_norm(x, lp['ffn1_norm_pre']), lp['ffn1_w_gu'], lp['ffn1_w_down']),
                               lp['ffn1_norm_post'])
        x = x + rms_norm(mixer(rms_norm(x, lp['mix_norm_pre']), lp), lp['mix_norm_post'])
        x = x + 0.5 * rms_norm(swiglu(rms_norm(x, lp['ffn2_norm_pre']), lp['ffn2_w_gu'], lp['ffn2_w_down']),
                               lp['ffn2_norm_post'])
    return x


def setup_inputs(seed: int = 0) -> dict:
    key = jax.random.key(seed)
    keys = iter(jax.random.split(key, 48))
    f32 = jnp.float32

    def nrm(shape, scale):
        return scale * jax.random.normal(next(keys), shape, f32)

    def gain(n):
        return 1.0 + nrm((DEPTH, n), 0.02)

    xbc_w = M2_W + 2 * M2_G * M2_N
    dt = jnp.exp(jax.random.uniform(next(keys), (DEPTH, 2 * M2_H), f32, math.log(1e-3), math.log(1e-1)))
    dt_bias = dt + jnp.log(-jnp.expm1(-dt))
    a_log = jnp.log(jax.random.uniform(next(keys), (DEPTH, 2 * M2_H), f32, 1.0, 16.0))
    decay = jnp.linspace(HY_DECAY_MIN, HY_DECAY_MAX, 4 * HY_C, dtype=f32)[None, :] * (1.0 + nrm((DEPTH, 4 * HY_C), 0.05))
    return {
        'x_prompt': nrm((BATCH, SEQ, D_MODEL), 1.0),
        'x_sample': nrm((DEC_BATCH, DEC_SEQ, D_MODEL), 1.0),
        'ffn1_norm_pre': gain(D_MODEL),
        'ffn1_norm_post': gain(D_MODEL),
        'ffn1_w_gu': nrm((DEPTH, D_MODEL, 2 * D_FF), D_MODEL ** -0.5),
        'ffn1_w_down': nrm((DEPTH, D_FF, D_MODEL), D_FF ** -0.5),
        'mix_norm_pre': gain(D_MODEL),
        'mix_norm_post': gain(D_MODEL),
        'w_in': nrm((DEPTH, D_MODEL, N_IN), D_MODEL ** -0.5),
        'hy_conv_w': nrm((DEPTH, SHORT_CONV, 3 * HY_C), SHORT_CONV ** -0.5),
        'hy_conv_b': nrm((DEPTH, 3 * HY_C), 0.02),
        'hy_filt_w1': nrm((DEPTH, HY_EMB, HY_FH), HY_EMB ** -0.5),
        'hy_filt_b1': nrm((DEPTH, HY_FH), 0.1),
        'hy_filt_w2': nrm((DEPTH, HY_FH, HY_FH), HY_FH ** -0.5),
        'hy_filt_b2': nrm((DEPTH, HY_FH), 0.1),
        'hy_filt_freq': 1.0 + nrm((DEPTH, 2, HY_FH), 0.05),
        'hy_filt_w3': nrm((DEPTH, HY_FH, 4 * HY_C), 0.1 * HY_FH ** -0.5),
        'hy_decay': decay,
        'hy_bias': nrm((DEPTH, HY_ORDER, HY_C), 0.1),
        'm2_conv_w': nrm((DEPTH, SHORT_CONV, xbc_w), SHORT_CONV ** -0.5),
        'm2_conv_b': nrm((DEPTH, xbc_w), 0.02),
        'm2_dt_bias': dt_bias,
        'm2_A_log': a_log,
        'm2_D': 1.0 + nrm((DEPTH, M2_H), 0.1),
        'm2_norm': gain(M2_W),
        'gla_w_gate2': nrm((DEPTH, 2, GLA_R, GLA_H * GLA_DK), GLA_R ** -0.5),
        'gla_b_gate': nrm((DEPTH, 2, GLA_H * GLA_DK), 0.1),
        'gla_norm': gain(GLA_DV),
        'hg_lb_param': nrm((DEPTH, 2, HG_H * HG_DK), 1.0),
        'hg_norm': gain(HG_DV),
        'w_branch': nrm((DEPTH, N_BRANCH, BR_W, D_MODEL), BR_W ** -0.5),
        'w_out': nrm((DEPTH, D_MODEL, D_MODEL), D_MODEL ** -0.5),
        'ffn2_norm_pre': gain(D_MODEL),
        'ffn2_norm_post': gain(D_MODEL),
        'ffn2_w_gu': nrm((DEPTH, D_MODEL, 2 * D_FF), D_MODEL ** -0.5),
        'ffn2_w_down': nrm((DEPTH, D_FF, D_MODEL), D_FF ** -0.5),
    }


def reference(x_prompt, x_sample,
              ffn1_norm_pre, ffn1_norm_post, ffn1_w_gu, ffn1_w_down,
              mix_norm_pre, mix_norm_post, w_in,
              hy_conv_w, hy_conv_b, hy_filt_w1, hy_filt_b1, hy_filt_w2, hy_filt_b2,
              hy_filt_freq, hy_filt_w3, hy_decay, hy_bias,
              m2_conv_w, m2_conv_b, m2_dt_bias, m2_A_log, m2_D, m2_norm,
              gla_w_gate2, gla_b_gate, gla_norm,
              hg_lb_param, hg_norm,
              w_branch, w_out,
              ffn2_norm_pre, ffn2_norm_post, ffn2_w_gu, ffn2_w_down):
    pl = jax.nn.softmax(hg_lb_param.astype(jnp.float32), axis=0)
    hg_lb = jnp.cumsum(pl, axis=0) - pl[0]
    params = {
        'ffn1_norm_pre': ffn1_norm_pre, 'ffn1_norm_post': ffn1_norm_post,
        'ffn1_w_gu': ffn1_w_gu, 'ffn1_w_down': ffn1_w_down,
        'mix_norm_pre': mix_norm_pre, 'mix_norm_post': mix_norm_post, 'w_in': w_in,
        'hy_conv_w': hy_conv_w, 'hy_conv_b': hy_conv_b,
        'hy_filt_w1': hy_filt_w1, 'hy_filt_b1': hy_filt_b1, 'hy_filt_w2': hy_filt_w2, 'hy_filt_b2': hy_filt_b2,
        'hy_filt_freq': hy_filt_freq, 'hy_filt_w3': hy_filt_w3, 'hy_decay': hy_decay, 'hy_bias': hy_bias,
        'm2_conv_w': m2_conv_w, 'm2_conv_b': m2_conv_b, 'm2_dt_bias': m2_dt_bias, 'm2_A_log': m2_A_log,
        'm2_D': m2_D, 'm2_norm': m2_norm,
        'gla_w_gate2': gla_w_gate2, 'gla_b_gate': gla_b_gate, 'gla_norm': gla_norm,
        'hg_lb': hg_lb, 'hg_norm': hg_norm,
        'w_branch': w_branch, 'w_out': w_out,
        'ffn2_norm_pre': ffn2_norm_pre, 'ffn2_norm_post': ffn2_norm_post,
        'ffn2_w_gu': ffn2_w_gu, 'ffn2_w_down': ffn2_w_down,
    }
    y_prompt = encoder(x_prompt, params)
    y_sample = encoder(x_sample, params)
    return (y_prompt, y_sample)
```

```python
import functools
import math

import numpy as np
import jax
import jax.numpy as jnp
from jax import lax
from jax.experimental import pallas as pl
from jax.experimental.pallas import tpu as pltpu

F32 = jnp.float32
BF16 = jnp.bfloat16
EPS = 1e-6
HIGHEST = lax.Precision.HIGHEST

SUBLANES = 8
LANES = 128
VMEM_BYTES = 64 * 1024 * 1024

D_MODEL = 1024
BR_W = 256
D_FF = 2816
HY_C = 256
HY_BANDS = 16
HY_EMB = 33
HY_FH = 64
M2_H = 4
M2_P = 64
M2_N = 64
GLA_H = 4
GLA_DK = 32
GLA_DV = 64
GLA_R = 16
GLA_GATE_NORM = 16.0
HG_H = 4
HG_DK = 64
HG_DV = 64
N_GATE = 4 * D_MODEL
O_HY, O_M2Z, O_XBC, O_DT, O_GQ, O_GK, O_GV, O_GA, O_GG, O_HQ, O_HF, O_HI, O_HG, O_GATE = (
    0, 768, 1024, 1536, 1544, 1672, 1800, 2056, 2088, 2344, 2600, 3112, 3368, 3624)
N_CONV = 768 + 512
SM_GA = 0
SM_DT = 32


def _cparams(sem, vmem_mb):
    return pltpu.CompilerParams(dimension_semantics=sem, vmem_limit_bytes=vmem_mb * 1024 * 1024)


def _dot(a, b):
    return jnp.dot(a, b, preferred_element_type=F32)


def _dot_nt(a, b):
    return lax.dot_general(a, b, (((1,), (1,)), ((), ())), preferred_element_type=F32)


def _dot_tn(a, b):
    return lax.dot_general(a, b, (((0,), (0,)), ((), ())), preferred_element_type=F32)


def _rms(x, w):
    return x * lax.rsqrt(jnp.mean(x * x, axis=-1, keepdims=True) + EPS) * w


def _silu(x):
    return x * jax.nn.sigmoid(x)


def _const_spec(shape):
    nd = len(shape)
    return pl.BlockSpec(shape, lambda *_: (0,) * nd, pipeline_mode=pl.Buffered(1))


FFN_TM = 512
FFN_FC = 256


def _ffn_kernel(x_ref, npre_ref, npost_ref, wgu_ref, wd_ref, o_ref):
    x = x_ref[0]
    xn = _rms(x, npre_ref[...]).astype(BF16)
    acc = jnp.zeros(x.shape, F32)
    for j in range(D_FF // FFN_FC):
        g = _dot(xn, wgu_ref[:, j * FFN_FC:(j + 1) * FFN_FC])
        u = _dot(xn, wgu_ref[:, D_FF + j * FFN_FC:D_FF + (j + 1) * FFN_FC])
        h = (_silu(g) * u).astype(BF16)
        acc = acc + _dot(h, wd_ref[j * FFN_FC:(j + 1) * FFN_FC, :])
    o_ref[0] = x + 0.5 * _rms(acc, npost_ref[...])


def _ffn(x, npre, npost, wgu, wd):
    b, l, d = x.shape
    tm = min(FFN_TM, l)
    xspec = pl.BlockSpec((1, tm, d), lambda i, j: (i, j, 0))
    return pl.pallas_call(
        _ffn_kernel,
        grid=(b, l // tm),
        in_specs=[xspec, _const_spec((1, d)), _const_spec((1, d)),
                  _const_spec(wgu.shape), _const_spec(wd.shape)],
        out_specs=xspec,
        out_shape=jax.ShapeDtypeStruct(x.shape, F32),
        compiler_params=_cparams(("parallel", "parallel"), 48),
        name="ffn",
    )(x, npre.reshape(1, d), npost.reshape(1, d), wgu, wd)


INP_TM = 512
_INP_OUTS = (("m2z", 256, BF16), ("small", 128, F32), ("gqk", 256, BF16), ("gv", 256, BF16), ("gg", 256, BF16),
             ("hq", 256, BF16), ("hf", 512, F32), ("hi", 256, BF16), ("hgg", 256, BF16))
N_INP = N_CONV + sum(w for _, w, _ in _INP_OUTS)


def _inproj_kernel(x_ref, xp_ref, xn_ref, nw_ref, w_ref, cw_ref, cb_ref, hy_ref, xbc_ref, *rest):
    i = pl.program_id(1)
    n = pl.num_programs(1)
    nw = nw_ref[...]
    h = _rms(x_ref[0], nw).astype(BF16)
    tm = h.shape[0]
    hp = _rms(xp_ref[0], nw).astype(BF16)
    hn = _rms(xn_ref[0], nw).astype(BF16)
    wc = w_ref[:, :N_CONV]
    p = _dot(h, wc)
    pp = _dot(hp, wc)[SUBLANES - 1:SUBLANES]
    pn = _dot(hn, wc)[0:1]
    pp = jnp.where(i > 0, pp, 0.0)
    pn = jnp.where(i < n - 1, pn, 0.0)
    row = lax.broadcasted_iota(jnp.int32, (tm, 1), 0)
    prev = jnp.where(row == 0, pp, pltpu.roll(p, 1, 0))
    nxt = jnp.where(row == tm - 1, pn, pltpu.roll(p, tm - 1, 0))
    cw = cw_ref[...]
    u = cb_ref[...] + prev * cw[0:1] + p * cw[1:2] + nxt * cw[2:3]
    hy_ref[0] = u[:, :768].astype(hy_ref.dtype)
    xbc_ref[0] = _silu(u[:, 768:]).astype(xbc_ref.dtype)
    off = N_CONV
    for (_, width, _), ref in zip(_INP_OUTS, rest):
        ref[0] = _dot(h, w_ref[:, off:off + width]).astype(ref.dtype)
        off += width


def _inproj(x, nw, w_p, conv_w, conv_b):
    b, l, d = x.shape
    tm = min(INP_TM, l)
    r8 = tm // SUBLANES
    nb8 = l // SUBLANES
    outs = (("hy", 768, BF16), ("xbc", 512, BF16)) + _INP_OUTS
    return pl.pallas_call(
        _inproj_kernel,
        grid=(b, l // tm),
        in_specs=[pl.BlockSpec((1, tm, d), lambda i, j: (i, j, 0)),
                  pl.BlockSpec((1, SUBLANES, d), lambda i, j: (i, jnp.maximum(j * r8 - 1, 0), 0)),
                  pl.BlockSpec((1, SUBLANES, d), lambda i, j: (i, jnp.minimum((j + 1) * r8, nb8 - 1), 0)),
                  _const_spec((1, d)), _const_spec(w_p.shape),
                  _const_spec(conv_w.shape), _const_spec(conv_b.shape)],
        out_specs=[pl.BlockSpec((1, tm, w), lambda i, j: (i, j, 0)) for _, w, _ in outs],
        out_shape=[jax.ShapeDtypeStruct((b, l, w), dt) for _, w, dt in outs],
        compiler_params=_cparams(("parallel", "parallel"), 48),
        name="inproj",
    )(x, x, x, nw.reshape(1, d), w_p, conv_w, conv_b)


MRG_TM = 512


def _merge_kernel(x_ref, ya_ref, yb_ref, yc_ref, yd_ref, npre_ref, npost_ref, wg_ref, wb_ref, wo_ref, o_ref):
    x = x_ref[0]
    h = _rms(x, npre_ref[...]).astype(BF16)
    merged = jnp.zeros(x.shape, F32)
    for n, y_ref in enumerate((ya_ref, yb_ref, yc_ref, yd_ref)):
        gate = jax.nn.sigmoid(_dot(h, wg_ref[:, n * D_MODEL:(n + 1) * D_MODEL]))
        merged = merged + gate * _dot(y_ref[0].astype(BF16), wb_ref[n])
    out = _dot(merged.astype(BF16), wo_ref[...])
    o_ref[0] = x + _rms(out, npost_ref[...])


def _merge(x, ys, npre, npost, w_gate, w_branch, w_out):
    b, l, d = x.shape
    tm = min(MRG_TM, l)
    xspec = pl.BlockSpec((1, tm, d), lambda i, j: (i, j, 0))
    yspec = pl.BlockSpec((1, tm, BR_W), lambda i, j: (i, j, 0))
    return pl.pallas_call(
        _merge_kernel,
        grid=(b, l // tm),
        in_specs=[xspec, yspec, yspec, yspec, yspec, _const_spec((1, d)), _const_spec((1, d)),
                  _const_spec(w_gate.shape), _const_spec(w_branch.shape), _const_spec(w_out.shape)],
        out_specs=xspec,
        out_shape=jax.ShapeDtypeStruct(x.shape, F32),
        compiler_params=_cparams(("parallel", "parallel"), 48),
        name="merge",
    )(x, *ys, npre.reshape(1, d), npost.reshape(1, d), w_gate, w_branch, w_out)


def _split_dot_left(m_bf16, x, passes):
    acc = None
    r = x
    for p in range(passes):
        piece = r.astype(BF16)
        t = _dot(m_bf16, piece)
        acc = t if acc is None else acc + t
        if p + 1 < passes:
            r = r - piece.astype(F32)
    return acc


def _split_dot_right(x, m_bf16, passes):
    acc = None
    r = x
    for p in range(passes):
        piece = r.astype(BF16)
        t = _dot(piece, m_bf16)
        acc = t if acc is None else acc + t
        if p + 1 < passes:
            r = r - piece.astype(F32)
    return acc


def _log_sigmoid(x):
    return jnp.minimum(x, 0.0) - jnp.log1p(jnp.exp(-jnp.abs(x)))


def _softplus(x):
    return jnp.maximum(x, 0.0) + jnp.log1p(jnp.exp(-jnp.abs(x)))


def _ones_where(mask):
    return jnp.where(mask, 1.0, 0.0).astype(BF16)


def _head_norm(o, nw_row, dv):
    cv = o.shape[1]
    r = lax.broadcasted_iota(jnp.int32, (cv, cv), 0) // dv
    s = lax.broadcasted_iota(jnp.int32, (cv, cv), 1) // dv
    ms = _split_dot_right(o * o, _ones_where(r == s), 2) * (1.0 / dv)
    return o * lax.rsqrt(ms + EPS) * nw_row


SCAN_TB = 128
SCAN_C = 16
NEG_BIG = -1e30


def _gla_block(q, k, v, la, st_ref, *, rev, dk, dv):
    tb, ck = q.shape
    cv = v.shape[1]
    c = SCAN_C
    nsub = tb // c
    r = lax.broadcasted_iota(jnp.int32, (tb, tb), 0)
    s = lax.broadcasted_iota(jnp.int32, (tb, tb), 1)
    same = (r // c) == (s // c)
    if rev:
        incl, excl = same & (s >= r), same & (s < r)
    else:
        incl, excl = same & (s <= r), same & (s > r)
    cb = _split_dot_left(_ones_where(incl), la, 3)
    ce = _split_dot_left(_ones_where(excl), la, 3)
    qe = (q * jnp.exp(cb)).astype(BF16)
    ke = (k * jnp.exp(ce)).astype(BF16)
    vb = v.astype(BF16)
    hmask = (lax.broadcasted_iota(jnp.int32, (cv, ck), 0) // dv) == (lax.broadcasted_iota(jnp.int32, (cv, ck), 1) // dk)
    st = st_ref[...]
    outs = [None] * nsub
    for i in (reversed(range(nsub)) if rev else range(nsub)):
        sl = slice(i * c, (i + 1) * c)
        outs[i] = _dot_nt(qe[sl], st.astype(BF16))
        upd = _dot_tn(vb[sl], ke[sl])
        edge = i * c if rev else (i + 1) * c - 1
        st = st * jnp.exp(cb[edge:edge + 1, :]) + jnp.where(hmask, upd, 0.0)
    st_ref[...] = st
    o = jnp.concatenate(outs, axis=0)
    mexp = _ones_where((lax.broadcasted_iota(jnp.int32, (ck, cv), 0) // dk)
                       == (lax.broadcasted_iota(jnp.int32, (ck, cv), 1) // dv))
    pos = lax.broadcasted_iota(jnp.int32, (tb, 1), 0) % c
    for dl in range(c):
        if dl == 0:
            p = q * k
            v_s = v
        else:
            sh = tb - dl if rev else dl
            valid = (pos < c - dl) if rev else (pos >= dl)
            e = jnp.exp(jnp.where(valid, cb - pltpu.roll(cb, sh, 0), NEG_BIG))
            p = q * pltpu.roll(k, sh, 0) * e
            v_s = pltpu.roll(v, sh, 0)
        o = o + _dot(p.astype(BF16), mexp) * v_s
    return o


def _gla_kernel(*refs, rev, final):
    if final:
        qk_ref, v_ref, sm_ref, wg_ref, bg_ref, oprev_ref, g_ref, nw_ref, o_ref, st_ref = refs
    else:
        qk_ref, v_ref, sm_ref, wg_ref, bg_ref, o_ref, st_ref = refs

    @pl.when(pl.program_id(1) == 0)
    def _():
        st_ref[...] = jnp.zeros(st_ref.shape, F32)

    ck = GLA_H * GLA_DK
    qk = qk_ref[0].astype(F32)
    q = qk[:, :ck] * (GLA_DK ** -0.5)
    k = qk[:, ck:]
    v = v_ref[0].astype(F32)
    pre = jnp.dot(sm_ref[0], wg_ref[...], preferred_element_type=F32, precision=HIGHEST) + bg_ref[...]
    la = _log_sigmoid(pre) * (1.0 / GLA_GATE_NORM)
    o = _gla_block(q, k, v, la, st_ref, rev=rev, dk=GLA_DK, dv=GLA_DV)
    if final:
        o = o + oprev_ref[0]
        o_ref[0] = (_head_norm(o, nw_ref[...], GLA_DV) * _silu(g_ref[0].astype(F32))).astype(o_ref.dtype)
    else:
        o_ref[0] = o


def _hg_kernel(*refs, rev, final):
    if final:
        q_ref, f_ref, i_ref, lb_ref, oprev_ref, g_ref, nw_ref, o_ref, st_ref = refs
    else:
        q_ref, f_ref, i_ref, lb_ref, o_ref, st_ref = refs

    @pl.when(pl.program_id(1) == 0)
    def _():
        st_ref[...] = jnp.zeros(st_ref.shape, F32)

    q = q_ref[0].astype(F32) * (HG_DK ** -0.5)
    z = f_ref[0]
    lb = lb_ref[...]
    la = _log_sigmoid(z) + jnp.log1p(lb * jnp.exp(-z))
    k = (1.0 - lb) * jax.nn.sigmoid(-z)
    v = i_ref[0].astype(F32)
    o = _gla_block(q, k, v, la, st_ref, rev=rev, dk=HG_DK, dv=HG_DV)
    if final:
        o = o + oprev_ref[0]
        o_ref[0] = (_head_norm(o, nw_ref[...], HG_DV) * _silu(g_ref[0].astype(F32))).astype(o_ref.dtype)
    else:
        o_ref[0] = o


def _scan_call(body, name, seq_ins, const_ins, fin_seq, fin_const, state_shape, b, l, rev, lane_block=None):
    tb = min(SCAN_TB, l)
    nblk = l // tb
    final = fin_seq is not None

    def seq_spec(width, lane_idx):
        if rev:
            return pl.BlockSpec((1, tb, width), lambda i, j: (i, nblk - 1 - j, lane_idx))
        return pl.BlockSpec((1, tb, width), lambda i, j: (i, j, lane_idx))

    arrays = [a for a, _, _ in seq_ins] + list(const_ins)
    specs = [seq_spec(w, li) for _, w, li in seq_ins] + [_const_spec(a.shape) for a in const_ins]
    if final:
        arrays += [a for a, _, _ in fin_seq] + list(fin_const)
        specs += [seq_spec(w, li) for _, w, li in fin_seq] + [_const_spec(a.shape) for a in fin_const]
    return pl.pallas_call(
        functools.partial(body, rev=rev, final=final),
        grid=(b, nblk),
        in_specs=specs,
        out_specs=seq_spec(BR_W, 0),
        out_shape=jax.ShapeDtypeStruct((b, l, BR_W), BF16 if final else F32),
        scratch_shapes=[pltpu.VMEM(state_shape, F32)],
        compiler_params=_cparams(("parallel", "arbitrary"), 32),
        name=name,
    )(*arrays)


def _gla_branch(gqk, gv, small, gg, wg2, bg, nw):
    b, l, _ = gqk.shape
    seq = [(gqk, 256, 0), (gv, 256, 0), (small, 128, 0)]
    o_f = _scan_call(_gla_kernel, "gla_fwd", seq, [wg2[0], bg[0]], None, None, (BR_W, GLA_H * GLA_DK), b, l, False)
    return _scan_call(_gla_kernel, "gla_bwd", seq, [wg2[1], bg[1]], [(o_f, 256, 0), (gg, 256, 0)], [nw],
                      (BR_W, GLA_H * GLA_DK), b, l, True)


def _hg_branch(hq, hf, hi, hgg, lb, nw):
    b, l, _ = hq.shape
    o_f = _scan_call(_hg_kernel, "hgrn_fwd", [(hq, 256, 0), (hf, 256, 0), (hi, 256, 0)], [lb[0]], None, None,
                     (BR_W, HG_H * HG_DK), b, l, False)
    return _scan_call(_hg_kernel, "hgrn_bwd", [(hq, 256, 0), (hf, 256, 1), (hi, 256, 0)], [lb[1]],
                      [(o_f, 256, 0), (hgg, 256, 0)], [nw], (BR_W, HG_H * HG_DK), b, l, True)


def _ssd_kernel(*refs, rev, final):
    if final:
        xbc_ref, sm_ref, dtb_ref, a_ref, oprev_ref, z_ref, dskip_ref, nw_ref, o_ref, st_ref = refs
    else:
        xbc_ref, sm_ref, dtb_ref, a_ref, o_ref, st_ref = refs

    @pl.when(pl.program_id(1) == 0)
    def _():
        st_ref[...] = jnp.zeros(st_ref.shape, F32)

    xbc = xbc_ref[0].astype(F32)
    tb = xbc.shape[0]
    x = xbc[:, :BR_W]
    bm = xbc[:, BR_W:BR_W + 2 * M2_N].astype(BF16)
    cm = xbc[:, BR_W + 2 * M2_N:]
    dt = _softplus(sm_ref[0] + dtb_ref[...])
    da = dt * a_ref[...]
    lane0 = SM_DT + (M2_H if rev else 0)

    t_i = lax.broadcasted_iota(jnp.int32, (tb, tb), 0)
    s_i = lax.broadcasted_iota(jnp.int32, (tb, tb), 1)
    if rev:
        incl, strict = s_i >= t_i, t_i < s_i
    else:
        incl, strict = s_i <= t_i, t_i > s_i
    m_incl = _ones_where(incl)
    cum = _split_dot_left(m_incl, da, 3)
    suf = _split_dot_left(_ones_where(~incl), da, 3)
    expand = _ones_where((lax.broadcasted_iota(jnp.int32, (LANES, BR_W), 0) - lane0)
                         == (lax.broadcasted_iota(jnp.int32, (LANES, BR_W), 1) // M2_P))
    dt_x = _split_dot_right(dt, expand, 2)
    ecum_x = _split_dot_right(jnp.exp(cum), expand, 2)
    w_x = _split_dot_right(dt * jnp.exp(suf), expand, 2)
    xdt = (x * dt_x).astype(BF16)
    xw = (x * w_x).astype(BF16)

    lane_g = lax.broadcasted_iota(jnp.int32, (1, 2 * M2_N), 1) // M2_N
    lane_h = lax.broadcasted_iota(jnp.int32, (1, BR_W), 1) // M2_P
    gram = [_dot_nt(jnp.where(lane_g == g, cm, 0.0).astype(BF16), bm) for g in range(2)]
    y = jnp.zeros((tb, BR_W), F32)
    for h in range(M2_H):
        col = jnp.broadcast_to(da[:, lane0 + h:lane0 + h + 1], (tb, tb))
        e_h = _split_dot_left(m_incl, jnp.where(strict, col, 0.0), 3)
        seg = jnp.exp(jnp.where(incl, e_h, NEG_BIG))
        scores = (gram[h // (M2_H // 2)] * seg).astype(BF16)
        y = y + jnp.where(lane_h == h, _dot(scores, xdt), 0.0)
    st = st_ref[...]
    y = y + ecum_x * _dot(cm.astype(BF16), st.astype(BF16))
    edge = 0 if rev else tb - 1
    gmask = ((lax.broadcasted_iota(jnp.int32, st.shape, 0) // M2_N)
             == (lax.broadcasted_iota(jnp.int32, st.shape, 1) // (2 * M2_P)))
    st_ref[...] = st * ecum_x[edge:edge + 1, :] + jnp.where(gmask, _dot_tn(bm, xw), 0.0)
    if final:
        y = y + oprev_ref[0] + dskip_ref[...] * x
        o_ref[0] = _rms(y * _silu(z_ref[0].astype(F32)), nw_ref[...]).astype(o_ref.dtype)
    else:
        o_ref[0] = y


def _ssd_branch(xbc, small, m2z, dtb_row, a_row, dskip_row, nw):
    b, l, _ = xbc.shape
    seq = [(xbc, 512, 0), (small, 128, 0)]
    shape = (2 * M2_N, BR_W)
    o_f = _scan_call(_ssd_kernel, "ssd_fwd", seq, [dtb_row, a_row], None, None, shape, b, l, False)
    return _scan_call(_ssd_kernel, "ssd_bwd", seq, [dtb_row, a_row], [(o_f, 256, 0), (m2z, 256, 0)],
                      [dskip_row, nw], shape, b, l, True)


FFT_N2 = 128
FFT_K1C = 16
HYF_TL = 512


def _hyfilt_kernel(z_ref, w1_ref, b1_ref, w2_ref, b2_ref, fr_ref, w3_ref, dec_ref, o_ref):
    def hdot(a, b):
        return jnp.dot(a, b, preferred_element_type=F32, precision=HIGHEST)

    z = z_ref[...]
    h = jnp.sin(fr_ref[0:1, :] * (hdot(z, w1_ref[...]) + b1_ref[...]))
    h = jnp.sin(fr_ref[1:2, :] * (hdot(h, w2_ref[...]) + b2_ref[...]))
    o_ref[...] = hdot(h, w3_ref[...]) * jnp.exp(-hdot(z, dec_ref[...]))


def _hy_filter(zfeat, w1, b1, w2, b2, freq, w3, dec):
    l = zfeat.shape[0]
    tl = min(HYF_TL, l)
    nout = w3.shape[1]
    return pl.pallas_call(
        _hyfilt_kernel,
        grid=(l // tl,),
        in_specs=[pl.BlockSpec((tl, LANES), lambda i: (i, 0))] + [_const_spec(a.shape) for a in (w1, b1, w2, b2, freq, w3, dec)],
        out_specs=pl.BlockSpec((tl, nout), lambda i: (i, 0)),
        out_shape=jax.ShapeDtypeStruct((l, nout), F32),
        compiler_params=_cparams(("parallel",), 32),
        name="hyena_filter",
    )(zfeat, w1, b1, w2, b2, freq, w3, dec)


def _fft_tables(n1):
    n2 = FFT_N2
    n = n1 * n2
    k1 = np.arange(n1, dtype=np.float64)
    ang1 = 2.0 * np.pi * np.outer(k1, k1) / n1
    f1 = np.concatenate([np.cos(ang1), -np.sin(ang1)], axis=0)
    c3 = np.concatenate([np.cos(ang1), -np.sin(ang1)], axis=1)
    k2 = np.arange(n2, dtype=np.float64)
    freq = k1[:, None, None] + n1 * k2[None, :, None]
    ang2 = 2.0 * np.pi * freq * k2[None, None, :] / n
    g_re, g_im = np.cos(ang2), -np.sin(ang2)
    g = np.concatenate([g_re, g_im], axis=1)
    gi = np.concatenate([g_re.transpose(0, 2, 1), -g_im.transpose(0, 2, 1)], axis=1)
    return tuple(jnp.asarray(a, F32).astype(BF16) for a in (f1, c3, g, gi))


def _fft_kernel(*refs, conv, n1, nt1):
    n2 = FFT_N2
    if conv:
        (u_ref, gate_ref, bias_ref, hre_ref, him_ref, f1_ref, c3_ref, g_ref, gi_ref,
         o_ref, are_ref, aim_ref, ubuf_ref, y_ref) = refs
    else:
        u_ref, f1_ref, g_ref, ore_ref, oim_ref, are_ref, aim_ref = refs
    kc = pl.program_id(2)
    k1c = g_ref.shape[0]

    @pl.when(kc == 0)
    def _stage1():
        if conv:
            ubuf_ref[...] = u_ref[0].astype(F32)
        src = ubuf_ref if conv else u_ref
        f1 = f1_ref[...]

        def body(r, carry):
            xr = src[pl.ds(r, nt1, stride=n2), :].astype(BF16)
            m = _dot(f1, xr)
            are_ref[pl.ds(r, n1, stride=n2), :] = m[:n1]
            aim_ref[pl.ds(r, n1, stride=n2), :] = m[n1:]
            return carry

        lax.fori_loop(0, n2, body, 0)

    def slab(j, carry):
        row = pl.multiple_of(j * n2, n2)
        arow = pl.multiple_of((kc * k1c + j) * n2, n2)
        a = jnp.concatenate([are_ref[pl.ds(arow, n2), :], aim_ref[pl.ds(arow, n2), :]], axis=1).astype(BF16)
        m = _dot(g_ref[j], a)
        xre = m[:n2, :LANES] - m[n2:, LANES:]
        xim = m[:n2, LANES:] + m[n2:, :LANES]
        if conv:
            hre = hre_ref[pl.ds(row, n2), :]
            him = him_ref[pl.ds(row, n2), :]
            y = jnp.concatenate([xre * hre - xim * him, xre * him + xim * hre], axis=1).astype(BF16)
            m2 = _dot(gi_ref[j], y)
            are_ref[pl.ds(arow, n2), :] = m2[:n2, :LANES] - m2[n2:, LANES:]
            aim_ref[pl.ds(arow, n2), :] = m2[:n2, LANES:] + m2[n2:, :LANES]
        else:
            ore_ref[pl.ds(row, n2), :] = xre
            oim_ref[pl.ds(row, n2), :] = xim
        return carry

    lax.fori_loop(0, k1c, slab, 0)

    if conv:
        @pl.when(kc == pl.num_programs(2) - 1)
        def _stage3():
            c3 = c3_ref[...]

            def body(r, carry):
                d = jnp.concatenate([are_ref[pl.ds(r, n1, stride=n2), :], aim_ref[pl.ds(r, n1, stride=n2), :]],
                                    axis=0).astype(BF16)
                y_ref[pl.ds(r, nt1, stride=n2), :] = _dot(c3, d)
                return carry

            lax.fori_loop(0, n2, body, 0)
            conv_out = y_ref[...] * (1.0 / (n1 * n2)) + bias_ref[...] * ubuf_ref[...]
            o_ref[0] = (gate_ref[0].astype(F32) * conv_out).astype(o_ref.dtype)


def _fft_spectrum(g, tables):
    n, c = g.shape
    n1 = n // FFT_N2
    k1c = min(FFT_K1C, n1)
    f1, _, gt, _ = tables
    ospec = pl.BlockSpec((k1c * FFT_N2, LANES), lambda ct, i, kc: (kc, ct))
    return pl.pallas_call(
        functools.partial(_fft_kernel, conv=False, n1=n1, nt1=n1),
        grid=(c // LANES, 1, n1 // k1c),
        in_specs=[pl.BlockSpec((n, LANES), lambda ct, i, kc: (0, ct)), _const_spec(f1.shape),
                  pl.BlockSpec((k1c, 2 * FFT_N2, FFT_N2), lambda ct, i, kc: (kc, 0, 0))],
        out_specs=[ospec, ospec],
        out_shape=[jax.ShapeDtypeStruct((n, c), F32)] * 2,
        scratch_shapes=[pltpu.VMEM((n, LANES), F32), pltpu.VMEM((n, LANES), F32)],
        compiler_params=_cparams(("parallel", "parallel", "arbitrary"), 48),
        name="hyena_spectrum",
    )(g, f1, gt)


def _fft_conv(u, u_lane, gate, gate_lane, bias, hre, him, h_lane, tables):
    b, l, _ = u.shape
    n1 = 2 * l // FFT_N2
    nt1 = n1 // 2
    k1c = min(FFT_K1C, n1)
    f1, c3, gt, git = tables
    f1 = f1[:, :nt1]
    c3 = c3[:nt1]
    hspec = pl.BlockSpec((k1c * FFT_N2, LANES), lambda ct, i, kc: (kc, h_lane + ct))
    tspec = pl.BlockSpec((k1c, 2 * FFT_N2, FFT_N2), lambda ct, i, kc: (kc, 0, 0))
    return pl.pallas_call(
        functools.partial(_fft_kernel, conv=True, n1=n1, nt1=nt1),
        grid=(BR_W // LANES, b, n1 // k1c),
        in_specs=[pl.BlockSpec((1, l, LANES), lambda ct, i, kc: (i, 0, u_lane + ct)),
                  pl.BlockSpec((1, l, LANES), lambda ct, i, kc: (i, 0, gate_lane + ct)),
                  pl.BlockSpec((1, LANES), lambda ct, i, kc: (0, ct)),
                  hspec, hspec, _const_spec(f1.shape), _const_spec(c3.shape), tspec, tspec],
        out_specs=pl.BlockSpec((1, l, LANES), lambda ct, i, kc: (i, 0, ct)),
        out_shape=jax.ShapeDtypeStruct((b, l, BR_W), BF16),
        scratch_shapes=[pltpu.VMEM((2 * l, LANES), F32), pltpu.VMEM((2 * l, LANES), F32),
                        pltpu.VMEM((l, LANES), F32), pltpu.VMEM((l, LANES), F32)],
        compiler_params=_cparams(("parallel", "parallel", "arbitrary"), 56),
        name="hyena_conv",
    )(u, gate, bias, hre, him, f1, c3, gt, git)


def _hyena_features(l):
    t = np.linspace(0.0, 1.0, l)[:, None]
    bands = np.linspace(1e-4, HY_BANDS - 1, HY_BANDS)[None, :]
    w = (2.0 * np.pi / l) * np.arange(l)[:, None]
    z = np.concatenate([t, np.cos(bands * w), -np.sin(bands * w)], axis=-1)
    return jnp.asarray(np.pad(z, ((0, 0), (0, LANES - HY_EMB))), F32)


def _hyena_branch(hy, lp, tables):
    b, l, _ = hy.shape
    pad = LANES - HY_FH
    w1 = jnp.pad(lp['hy_filt_w1'], ((0, LANES - HY_EMB), (0, pad)))
    w2 = jnp.pad(lp['hy_filt_w2'], ((0, pad), (0, pad)))
    w3 = jnp.pad(lp['hy_filt_w3'], ((0, pad), (0, 0)))
    b1 = jnp.pad(lp['hy_filt_b1'], (0, pad)).reshape(1, LANES)
    b2 = jnp.pad(lp['hy_filt_b2'], (0, pad)).reshape(1, LANES)
    freq = jnp.pad(lp['hy_filt_freq'], ((0, 0), (0, pad)))
    dec = jnp.pad(jnp.abs(lp['hy_decay']).reshape(1, -1), ((0, LANES - 1), (0, 0)))
    filt = _hy_filter(_hyena_features(l), w1, b1, w2, b2, freq, w3, dec).reshape(l, 2, 2, HY_C)
    hf, hb = filt[:, :, 0], filt[:, :, 1]
    g = jnp.concatenate([hf, jnp.zeros((1, 2, HY_C), F32), jnp.flip(hb, axis=0)[:l - 1]], axis=0)
    hre, him = _fft_spectrum(g.reshape(2 * l, 2 * HY_C), tables)
    nlb = HY_C // LANES
    z = _fft_conv(hy, 0, hy, nlb, lp['hy_bias'][0:1], hre, him, 0, tables)
    return _fft_conv(z, 0, hy, 2 * nlb, lp['hy_bias'][1:2], hre, him, nlb, tables)


def _mixer(x, lp, tables):
    w = lp['w_in']
    zpad = jnp.zeros((D_MODEL, LANES - 2 * GLA_R - 2 * M2_H), w.dtype)
    w_p = jnp.concatenate(
        [w[:, O_HY:O_M2Z], w[:, O_XBC:O_DT], w[:, O_M2Z:O_XBC], w[:, O_GA:O_GG], w[:, O_DT:O_GQ], zpad,
         w[:, O_GQ:O_GV], w[:, O_GV:O_GA], w[:, O_GG:O_HQ], w[:, O_HQ:O_HF], w[:, O_HF:O_HI], w[:, O_HI:O_HG],
         w[:, O_HG:O_GATE]], axis=1).astype(BF16)
    conv_w = jnp.concatenate([lp['hy_conv_w'], lp['m2_conv_w']], axis=1)
    conv_b = jnp.concatenate([lp['hy_conv_b'], lp['m2_conv_b']]).reshape(1, N_CONV)
    hy, xbc, m2z, small, gqk, gv, gg, hq, hf, hi, hgg = _inproj(x, lp['mix_norm_pre'], w_p, conv_w, conv_b)

    y_a = _hyena_branch(hy, lp, tables)

    dt_lanes = slice(SM_DT, SM_DT + 2 * M2_H)
    dtb_row = jnp.zeros((1, LANES), F32).at[0, dt_lanes].set(lp['m2_dt_bias'])
    a_row = jnp.zeros((1, LANES), F32).at[0, dt_lanes].set(-jnp.exp(lp['m2_A_log'].astype(F32)))
    dskip_row = jnp.repeat(lp['m2_D'], M2_P).reshape(1, BR_W)
    y_b = _ssd_branch(xbc, small, m2z, dtb_row, a_row, dskip_row, lp['m2_norm'].reshape(1, BR_W))

    wg2 = jnp.zeros((2, LANES, GLA_H * GLA_DK), F32)
    for dr in range(2):
        wg2 = wg2.at[dr, SM_GA + dr * GLA_R:SM_GA + (dr + 1) * GLA_R].set(lp['gla_w_gate2'][dr])
    y_c = _gla_branch(gqk, gv, small, gg, wg2, lp['gla_b_gate'].reshape(2, 1, GLA_H * GLA_DK),
                      jnp.tile(lp['gla_norm'], GLA_H).reshape(1, BR_W))

    y_d = _hg_branch(hq, hf, hi, hgg, lp['hg_lb'].reshape(2, 1, HG_H * HG_DK),
                     jnp.tile(lp['hg_norm'], HG_H).reshape(1, BR_W))

    return _merge(x, (y_a, y_b, y_c, y_d), lp['mix_norm_pre'], lp['mix_norm_post'],
                  w[:, O_GATE:].astype(BF16), lp['w_branch'].astype(BF16), lp['w_out'].astype(BF16))


def kernel(x_prompt, x_sample, ffn1_norm_pre, ffn1_norm_post, ffn1_w_gu, ffn1_w_down, mix_norm_pre, mix_norm_post, w_in, hy_conv_w, hy_conv_b, hy_filt_w1, hy_filt_b1, hy_filt_w2, hy_filt_b2, hy_filt_freq, hy_filt_w3, hy_decay, hy_bias, m2_conv_w, m2_conv_b, m2_dt_bias, m2_A_log, m2_D, m2_norm, gla_w_gate2, gla_b_gate, gla_norm, hg_lb_param, hg_norm, w_branch, w_out, ffn2_norm_pre, ffn2_norm_post, ffn2_w_gu, ffn2_w_down):
    sm = jax.nn.softmax(hg_lb_param.astype(F32), axis=0)
    hg_lb = jnp.cumsum(sm, axis=0) - sm[0]
    params = {
        'mix_norm_pre': mix_norm_pre, 'mix_norm_post': mix_norm_post, 'w_in': w_in,
        'hy_conv_w': hy_conv_w, 'hy_conv_b': hy_conv_b,
        'hy_filt_w1': hy_filt_w1, 'hy_filt_b1': hy_filt_b1, 'hy_filt_w2': hy_filt_w2, 'hy_filt_b2': hy_filt_b2,
        'hy_filt_freq': hy_filt_freq, 'hy_filt_w3': hy_filt_w3, 'hy_decay': hy_decay, 'hy_bias': hy_bias,
        'm2_conv_w': m2_conv_w, 'm2_conv_b': m2_conv_b, 'm2_dt_bias': m2_dt_bias, 'm2_A_log': m2_A_log,
        'm2_D': m2_D, 'm2_norm': m2_norm,
        'gla_w_gate2': gla_w_gate2, 'gla_b_gate': gla_b_gate, 'gla_norm': gla_norm,
        'hg_lb': hg_lb, 'hg_norm': hg_norm, 'w_branch': w_branch, 'w_out': w_out,
    }
    nb = x_prompt.shape[0]
    x = jnp.concatenate([x_prompt, x_sample], axis=0)
    depth = w_in.shape[0]
    tables = _fft_tables(2 * x.shape[1] // FFT_N2)
    for layer in range(depth):
        lp = {name: arr[layer] for name, arr in params.items()}
        x = _ffn(x, ffn1_norm_pre[layer], ffn1_norm_post[layer],
                 ffn1_w_gu[layer].astype(BF16), ffn1_w_down[layer].astype(BF16))
        x = _mixer(x, lp, tables)
        x = _ffn(x, ffn2_norm_pre[layer], ffn2_norm_post[layer],
                 ffn2_w_gu[layer].astype(BF16), ffn2_w_down[layer].astype(BF16))
    return (x[:nb], x[nb:])
```

```python
import functools
import math

import numpy as np
import jax
import jax.numpy as jnp
from jax import lax
from jax.experimental import pallas as pl
from jax.experimental.pallas import tpu as pltpu

F32 = jnp.float32
BF16 = jnp.bfloat16
EPS = 1e-6
HIGHEST = lax.Precision.HIGHEST

SUBLANES = 8
LANES = 128
VMEM_BYTES = 64 * 1024 * 1024

D_MODEL = 1024
BR_W = 256
D_FF = 2816
HY_C = 256
HY_BANDS = 16
HY_EMB = 33
HY_FH = 64
M2_H = 4
M2_P = 64
M2_N = 64
GLA_H = 4
GLA_DK = 32
GLA_DV = 64
GLA_R = 16
GLA_GATE_NORM = 16.0
HG_H = 4
HG_DK = 64
HG_DV = 64
N_GATE = 4 * D_MODEL
O_HY, O_M2Z, O_XBC, O_DT, O_GQ, O_GK, O_GV, O_GA, O_GG, O_HQ, O_HF, O_HI, O_HG, O_GATE = (
    0, 768, 1024, 1536, 1544, 1672, 1800, 2056, 2088, 2344, 2600, 3112, 3368, 3624)
N_CONV = 768 + 512
SM_GA = 0
SM_DT = 32


def _cparams(sem, vmem_mb):
    return pltpu.CompilerParams(dimension_semantics=sem, vmem_limit_bytes=vmem_mb * 1024 * 1024)


def _dot(a, b):
    return jnp.dot(a, b, preferred_element_type=F32)


def _dot_nt(a, b):
    return lax.dot_general(a, b, (((1,), (1,)), ((), ())), preferred_element_type=F32)


def _dot_tn(a, b):
    return lax.dot_general(a, b, (((0,), (0,)), ((), ())), preferred_element_type=F32)


def _rms(x, w):
    return x * lax.rsqrt(jnp.mean(x * x, axis=-1, keepdims=True) + EPS) * w


def _silu(x):
    return x * jax.nn.sigmoid(x)


def _const_spec(shape):
    nd = len(shape)
    return pl.BlockSpec(shape, lambda *_: (0,) * nd, pipeline_mode=pl.Buffered(1))


FFN_TM = 512
FFN_FC = 256


def _ffn_kernel(x_ref, npre_ref, npost_ref, wgu_ref, wd_ref, o_ref):
    x = x_ref[0]
    xn = _rms(x, npre_ref[...]).astype(BF16)
    acc = jnp.zeros(x.shape, F32)
    for j in range(D_FF // FFN_FC):
        g = _dot(xn, wgu_ref[:, j * FFN_FC:(j + 1) * FFN_FC])
        u = _dot(xn, wgu_ref[:, D_FF + j * FFN_FC:D_FF + (j + 1) * FFN_FC])
        h = (_silu(g) * u).astype(BF16)
        acc = acc + _dot(h, wd_ref[j * FFN_FC:(j + 1) * FFN_FC, :])
    o_ref[0] = x + 0.5 * _rms(acc, npost_ref[...])


def _ffn(x, npre, npost, wgu, wd):
    b, l, d = x.shape
    tm = min(FFN_TM, l)
    xspec = pl.BlockSpec((1, tm, d), lambda i, j: (i, j, 0))
    return pl.pallas_call(
        _ffn_kernel,
        grid=(b, l // tm),
        in_specs=[xspec, _const_spec((1, d)), _const_spec((1, d)),
                  _const_spec(wgu.shape), _const_spec(wd.shape)],
        out_specs=xspec,
        out_shape=jax.ShapeDtypeStruct(x.shape, F32),
        compiler_params=_cparams(("parallel", "parallel"), 48),
        name="ffn",
    )(x, npre.reshape(1, d), npost.reshape(1, d), wgu, wd)


INP_TM = 512
_INP_OUTS = (("m2z", 256, BF16), ("small", 128, F32), ("gqk", 256, BF16), ("gv", 256, BF16), ("gg", 256, BF16),
             ("hq", 256, BF16), ("hf", 512, F32), ("hi", 256, BF16), ("hgg", 256, BF16))
N_HY = 3 * HY_C
N_XBC = BR_W + 4 * M2_N
N_INP = N_XBC + sum(w for _, w, _ in _INP_OUTS)


def _inproj_kernel(x_ref, xp_ref, xn_ref, nw_ref, wt_ref, cwt_ref, w_ref, cw_ref, cb_ref, hy_ref, xbc_ref, *rest):
    i = pl.program_id(1)
    n = pl.num_programs(1)
    nw = nw_ref[...]
    h = _rms(x_ref[0], nw).astype(BF16)
    tm = h.shape[0]
    hp = _rms(xp_ref[0], nw).astype(BF16)
    hn = _rms(xn_ref[0], nw).astype(BF16)
    first, last = i == 0, i == n - 1

    wt = wt_ref[...]
    p = _dot_nt(wt, h)
    pp = jnp.where(first, 0.0, _dot_nt(wt, hp)[:, SUBLANES - 1:SUBLANES])
    pn = jnp.where(last, 0.0, _dot_nt(wt, hn)[:, 0:1])
    lane = lax.broadcasted_iota(jnp.int32, (1, tm), 1)
    prev = jnp.where(lane == 0, pp, pltpu.roll(p, 1, 1))
    nxt = jnp.where(lane == tm - 1, pn, pltpu.roll(p, tm - 1, 1))
    cwt = cwt_ref[...]
    hy_ref[0] = (cwt[:, 3:4] + prev * cwt[:, 0:1] + p * cwt[:, 1:2] + nxt * cwt[:, 2:3]).astype(hy_ref.dtype)

    wc = w_ref[:, :N_XBC]
    p = _dot(h, wc)
    pp = jnp.where(first, 0.0, _dot(hp, wc)[SUBLANES - 1:SUBLANES])
    pn = jnp.where(last, 0.0, _dot(hn, wc)[0:1])
    row = lax.broadcasted_iota(jnp.int32, (tm, 1), 0)
    prev = jnp.where(row == 0, pp, pltpu.roll(p, 1, 0))
    nxt = jnp.where(row == tm - 1, pn, pltpu.roll(p, tm - 1, 0))
    cw = cw_ref[...]
    xbc_ref[0] = _silu(cb_ref[...] + prev * cw[0:1] + p * cw[1:2] + nxt * cw[2:3]).astype(xbc_ref.dtype)
    off = N_XBC
    for (_, width, _), ref in zip(_INP_OUTS, rest):
        ref[0] = _dot(h, w_ref[:, off:off + width]).astype(ref.dtype)
        off += width


def _inproj(x, nw, w_hyt, conv_hyt, w_p, conv_w, conv_b):
    b, l, d = x.shape
    tm = min(INP_TM, l)
    r8 = tm // SUBLANES
    nb8 = l // SUBLANES
    outs = (("xbc", N_XBC, BF16),) + _INP_OUTS
    return pl.pallas_call(
        _inproj_kernel,
        grid=(b, l // tm),
        in_specs=[pl.BlockSpec((1, tm, d), lambda i, j: (i, j, 0)),
                  pl.BlockSpec((1, SUBLANES, d), lambda i, j: (i, jnp.maximum(j * r8 - 1, 0), 0)),
                  pl.BlockSpec((1, SUBLANES, d), lambda i, j: (i, jnp.minimum((j + 1) * r8, nb8 - 1), 0)),
                  _const_spec((1, d)), _const_spec(w_hyt.shape), _const_spec(conv_hyt.shape), _const_spec(w_p.shape),
                  _const_spec(conv_w.shape), _const_spec(conv_b.shape)],
        out_specs=[pl.BlockSpec((1, N_HY, tm), lambda i, j: (i, 0, j))]
        + [pl.BlockSpec((1, tm, w), lambda i, j: (i, j, 0)) for _, w, _ in outs],
        out_shape=[jax.ShapeDtypeStruct((b, N_HY, l), BF16)]
        + [jax.ShapeDtypeStruct((b, l, w), dt) for _, w, dt in outs],
        compiler_params=_cparams(("parallel", "parallel"), 48),
        name="inproj",
    )(x, x, x, nw.reshape(1, d), w_hyt, conv_hyt, w_p, conv_w, conv_b)


MRG_TM = 512


def _merge_kernel(x_ref, ya_ref, yb_ref, yc_ref, yd_ref, npre_ref, npost_ref, wg_ref, wb_ref, wo_ref, o_ref):
    x = x_ref[0]
    h = _rms(x, npre_ref[...]).astype(BF16)
    merged = jnp.zeros(x.shape, F32)
    for n, y_ref in enumerate((ya_ref, yb_ref, yc_ref, yd_ref)):
        gate = jax.nn.sigmoid(_dot(h, wg_ref[:, n * D_MODEL:(n + 1) * D_MODEL]))
        y = y_ref[0].astype(BF16)
        proj = _dot_tn(y, wb_ref[n]) if n == 0 else _dot(y, wb_ref[n])
        merged = merged + gate * proj
    out = _dot(merged.astype(BF16), wo_ref[...])
    o_ref[0] = x + _rms(out, npost_ref[...])


def _merge(x, ys, npre, npost, w_gate, w_branch, w_out):
    b, l, d = x.shape
    tm = min(MRG_TM, l)
    xspec = pl.BlockSpec((1, tm, d), lambda i, j: (i, j, 0))
    yspec = pl.BlockSpec((1, tm, BR_W), lambda i, j: (i, j, 0))
    return pl.pallas_call(
        _merge_kernel,
        grid=(b, l // tm),
        in_specs=[xspec, pl.BlockSpec((1, BR_W, tm), lambda i, j: (i, 0, j)), yspec, yspec, yspec,
                  _const_spec((1, d)), _const_spec((1, d)),
                  _const_spec(w_gate.shape), _const_spec(w_branch.shape), _const_spec(w_out.shape)],
        out_specs=xspec,
        out_shape=jax.ShapeDtypeStruct(x.shape, F32),
        compiler_params=_cparams(("parallel", "parallel"), 48),
        name="merge",
    )(x, *ys, npre.reshape(1, d), npost.reshape(1, d), w_gate, w_branch, w_out)


def _split_dot_left(m_bf16, x, passes):
    acc = None
    r = x
    for p in range(passes):
        piece = r.astype(BF16)
        t = _dot(m_bf16, piece)
        acc = t if acc is None else acc + t
        if p + 1 < passes:
            r = r - piece.astype(F32)
    return acc


def _split_dot_right(x, m_bf16, passes):
    acc = None
    r = x
    for p in range(passes):
        piece = r.astype(BF16)
        t = _dot(piece, m_bf16)
        acc = t if acc is None else acc + t
        if p + 1 < passes:
            r = r - piece.astype(F32)
    return acc


def _log_sigmoid(x):
    return jnp.minimum(x, 0.0) - jnp.log1p(jnp.exp(-jnp.abs(x)))


def _softplus(x):
    return jnp.maximum(x, 0.0) + jnp.log1p(jnp.exp(-jnp.abs(x)))


def _ones_where(mask):
    return jnp.where(mask, 1.0, 0.0).astype(BF16)


def _head_norm(o, nw_row, dv):
    cv = o.shape[1]
    r = lax.broadcasted_iota(jnp.int32, (cv, cv), 0) // dv
    s = lax.broadcasted_iota(jnp.int32, (cv, cv), 1) // dv
    ms = _split_dot_right(o * o, _ones_where(r == s), 2) * (1.0 / dv)
    return o * lax.rsqrt(ms + EPS) * nw_row


SCAN_TB = 128
SCAN_C = 16
NEG_BIG = -1e30


def _gla_block(q, k, v, la, st_ref, *, rev, dk, dv):
    tb, ck = q.shape
    cv = v.shape[1]
    c = SCAN_C
    nsub = tb // c
    r = lax.broadcasted_iota(jnp.int32, (tb, tb), 0)
    s = lax.broadcasted_iota(jnp.int32, (tb, tb), 1)
    same = (r // c) == (s // c)
    if rev:
        incl, excl = same & (s >= r), same & (s < r)
    else:
        incl, excl = same & (s <= r), same & (s > r)
    cb = _split_dot_left(_ones_where(incl), la, 3)
    ce = _split_dot_left(_ones_where(excl), la, 3)
    qe = (q * jnp.exp(cb)).astype(BF16)
    ke = (k * jnp.exp(ce)).astype(BF16)
    vb = v.astype(BF16)
    hmask = (lax.broadcasted_iota(jnp.int32, (cv, ck), 0) // dv) == (lax.broadcasted_iota(jnp.int32, (cv, ck), 1) // dk)
    st = st_ref[...]
    outs = [None] * nsub
    for i in (reversed(range(nsub)) if rev else range(nsub)):
        sl = slice(i * c, (i + 1) * c)
        outs[i] = _dot_nt(qe[sl], st.astype(BF16))
        upd = _dot_tn(vb[sl], ke[sl])
        edge = i * c if rev else (i + 1) * c - 1
        st = st * jnp.exp(cb[edge:edge + 1, :]) + jnp.where(hmask, upd, 0.0)
    st_ref[...] = st
    o = jnp.concatenate(outs, axis=0)
    mexp = _ones_where((lax.broadcasted_iota(jnp.int32, (ck, cv), 0) // dk)
                       == (lax.broadcasted_iota(jnp.int32, (ck, cv), 1) // dv))
    pos = lax.broadcasted_iota(jnp.int32, (tb, 1), 0) % c
    for dl in range(c):
        if dl == 0:
            p = q * k
            v_s = v
        else:
            sh = tb - dl if rev else dl
            valid = (pos < c - dl) if rev else (pos >= dl)
            e = jnp.exp(jnp.where(valid, cb - pltpu.roll(cb, sh, 0), NEG_BIG))
            p = q * pltpu.roll(k, sh, 0) * e
            v_s = pltpu.roll(v, sh, 0)
        o = o + _dot(p.astype(BF16), mexp) * v_s
    return o


def _gla_kernel(*refs, rev, final):
    if final:
        qk_ref, v_ref, sm_ref, wg_ref, bg_ref, oprev_ref, g_ref, nw_ref, o_ref, st_ref = refs
    else:
        qk_ref, v_ref, sm_ref, wg_ref, bg_ref, o_ref, st_ref = refs

    @pl.when(pl.program_id(1) == 0)
    def _():
        st_ref[...] = jnp.zeros(st_ref.shape, F32)

    ck = GLA_H * GLA_DK
    qk = qk_ref[0].astype(F32)
    q = qk[:, :ck] * (GLA_DK ** -0.5)
    k = qk[:, ck:]
    v = v_ref[0].astype(F32)
    pre = jnp.dot(sm_ref[0], wg_ref[...], preferred_element_type=F32, precision=HIGHEST) + bg_ref[...]
    la = _log_sigmoid(pre) * (1.0 / GLA_GATE_NORM)
    o = _gla_block(q, k, v, la, st_ref, rev=rev, dk=GLA_DK, dv=GLA_DV)
    if final:
        o = o + oprev_ref[0]
        o_ref[0] = (_head_norm(o, nw_ref[...], GLA_DV) * _silu(g_ref[0].astype(F32))).astype(o_ref.dtype)
    else:
        o_ref[0] = o


def _hg_kernel(*refs, rev, final):
    if final:
        q_ref, f_ref, i_ref, lb_ref, oprev_ref, g_ref, nw_ref, o_ref, st_ref = refs
    else:
        q_ref, f_ref, i_ref, lb_ref, o_ref, st_ref = refs

    @pl.when(pl.program_id(1) == 0)
    def _():
        st_ref[...] = jnp.zeros(st_ref.shape, F32)

    q = q_ref[0].astype(F32) * (HG_DK ** -0.5)
    z = f_ref[0]
    lb = lb_ref[...]
    la = _log_sigmoid(z) + jnp.log1p(lb * jnp.exp(-z))
    k = (1.0 - lb) * jax.nn.sigmoid(-z)
    v = i_ref[0].astype(F32)
    o = _gla_block(q, k, v, la, st_ref, rev=rev, dk=HG_DK, dv=HG_DV)
    if final:
        o = o + oprev_ref[0]
        o_ref[0] = (_head_norm(o, nw_ref[...], HG_DV) * _silu(g_ref[0].astype(F32))).astype(o_ref.dtype)
    else:
        o_ref[0] = o


def _scan_call(body, name, seq_ins, const_ins, fin_seq, fin_const, state_shape, b, l, rev, lane_block=None):
    tb = min(SCAN_TB, l)
    nblk = l // tb
    final = fin_seq is not None

    def seq_spec(width, lane_idx):
        if rev:
            return pl.BlockSpec((1, tb, width), lambda i, j: (i, nblk - 1 - j, lane_idx))
        return pl.BlockSpec((1, tb, width), lambda i, j: (i, j, lane_idx))

    arrays = [a for a, _, _ in seq_ins] + list(const_ins)
    specs = [seq_spec(w, li) for _, w, li in seq_ins] + [_const_spec(a.shape) for a in const_ins]
    if final:
        arrays += [a for a, _, _ in fin_seq] + list(fin_const)
        specs += [seq_spec(w, li) for _, w, li in fin_seq] + [_const_spec(a.shape) for a in fin_const]
    return pl.pallas_call(
        functools.partial(body, rev=rev, final=final),
        grid=(b, nblk),
        in_specs=specs,
        out_specs=seq_spec(BR_W, 0),
        out_shape=jax.ShapeDtypeStruct((b, l, BR_W), BF16 if final else F32),
        scratch_shapes=[pltpu.VMEM(state_shape, F32)],
        compiler_params=_cparams(("parallel", "arbitrary"), 32),
        name=name,
    )(*arrays)


def _gla_branch(gqk, gv, small, gg, wg2, bg, nw):
    b, l, _ = gqk.shape
    seq = [(gqk, 256, 0), (gv, 256, 0), (small, 128, 0)]
    o_f = _scan_call(_gla_kernel, "gla_fwd", seq, [wg2[0], bg[0]], None, None, (BR_W, GLA_H * GLA_DK), b, l, False)
    return _scan_call(_gla_kernel, "gla_bwd", seq, [wg2[1], bg[1]], [(o_f, 256, 0), (gg, 256, 0)], [nw],
                      (BR_W, GLA_H * GLA_DK), b, l, True)


def _hg_branch(hq, hf, hi, hgg, lb, nw):
    b, l, _ = hq.shape
    o_f = _scan_call(_hg_kernel, "hgrn_fwd", [(hq, 256, 0), (hf, 256, 0), (hi, 256, 0)], [lb[0]], None, None,
                     (BR_W, HG_H * HG_DK), b, l, False)
    return _scan_call(_hg_kernel, "hgrn_bwd", [(hq, 256, 0), (hf, 256, 1), (hi, 256, 0)], [lb[1]],
                      [(o_f, 256, 0), (hgg, 256, 0)], [nw], (BR_W, HG_H * HG_DK), b, l, True)


def _ssd_kernel(*refs, rev, final):
    if final:
        xbc_ref, sm_ref, dtb_ref, a_ref, oprev_ref, z_ref, dskip_ref, nw_ref, o_ref, st_ref = refs
    else:
        xbc_ref, sm_ref, dtb_ref, a_ref, o_ref, st_ref = refs

    @pl.when(pl.program_id(1) == 0)
    def _():
        st_ref[...] = jnp.zeros(st_ref.shape, F32)

    xbc = xbc_ref[0].astype(F32)
    tb = xbc.shape[0]
    x = xbc[:, :BR_W]
    bm = xbc[:, BR_W:BR_W + 2 * M2_N].astype(BF16)
    cm = xbc[:, BR_W + 2 * M2_N:]
    dt = _softplus(sm_ref[0] + dtb_ref[...])
    da = dt * a_ref[...]
    lane0 = SM_DT + (M2_H if rev else 0)

    t_i = lax.broadcasted_iota(jnp.int32, (tb, tb), 0)
    s_i = lax.broadcasted_iota(jnp.int32, (tb, tb), 1)
    if rev:
        incl, strict = s_i >= t_i, t_i < s_i
    else:
        incl, strict = s_i <= t_i, t_i > s_i
    m_incl = _ones_where(incl)
    cum = _split_dot_left(m_incl, da, 3)
    suf = _split_dot_left(_ones_where(~incl), da, 3)
    expand = _ones_where((lax.broadcasted_iota(jnp.int32, (LANES, BR_W), 0) - lane0)
                         == (lax.broadcasted_iota(jnp.int32, (LANES, BR_W), 1) // M2_P))
    dt_x = _split_dot_right(dt, expand, 2)
    ecum_x = _split_dot_right(jnp.exp(cum), expand, 2)
    w_x = _split_dot_right(dt * jnp.exp(suf), expand, 2)
    xdt = (x * dt_x).astype(BF16)
    xw = (x * w_x).astype(BF16)

    lane_g = lax.broadcasted_iota(jnp.int32, (1, 2 * M2_N), 1) // M2_N
    lane_h = lax.broadcasted_iota(jnp.int32, (1, BR_W), 1) // M2_P
    gram = [_dot_nt(jnp.where(lane_g == g, cm, 0.0).astype(BF16), bm) for g in range(2)]
    y = jnp.zeros((tb, BR_W), F32)
    for h in range(M2_H):
        col = jnp.broadcast_to(da[:, lane0 + h:lane0 + h + 1], (tb, tb))
        e_h = _split_dot_left(m_incl, jnp.where(strict, col, 0.0), 3)
        seg = jnp.exp(jnp.where(incl, e_h, NEG_BIG))
        scores = (gram[h // (M2_H // 2)] * seg).astype(BF16)
        y = y + jnp.where(lane_h == h, _dot(scores, xdt), 0.0)
    st = st_ref[...]
    y = y + ecum_x * _dot(cm.astype(BF16), st.astype(BF16))
    edge = 0 if rev else tb - 1
    gmask = ((lax.broadcasted_iota(jnp.int32, st.shape, 0) // M2_N)
             == (lax.broadcasted_iota(jnp.int32, st.shape, 1) // (2 * M2_P)))
    st_ref[...] = st * ecum_x[edge:edge + 1, :] + jnp.where(gmask, _dot_tn(bm, xw), 0.0)
    if final:
        y = y + oprev_ref[0] + dskip_ref[...] * x
        o_ref[0] = _rms(y * _silu(z_ref[0].astype(F32)), nw_ref[...]).astype(o_ref.dtype)
    else:
        o_ref[0] = y


def _ssd_branch(xbc, small, m2z, dtb_row, a_row, dskip_row, nw):
    b, l, _ = xbc.shape
    seq = [(xbc, 512, 0), (small, 128, 0)]
    shape = (2 * M2_N, BR_W)
    o_f = _scan_call(_ssd_kernel, "ssd_fwd", seq, [dtb_row, a_row], None, None, shape, b, l, False)
    return _scan_call(_ssd_kernel, "ssd_bwd", seq, [dtb_row, a_row], [(o_f, 256, 0), (m2z, 256, 0)],
                      [dskip_row, nw], shape, b, l, True)


FFT_N2 = 128
FFT_CC = 16
HYF_TL = 512


def _hyfilt_kernel(z_ref, w1_ref, b1_ref, w2_ref, b2_ref, fr_ref, w3t_ref, dect_ref, o_ref):
    def hdot(a, b):
        return jnp.dot(a, b, preferred_element_type=F32, precision=HIGHEST)

    def hdot_nt(a, b):
        return lax.dot_general(a, b, (((1,), (1,)), ((), ())), preferred_element_type=F32, precision=HIGHEST)

    z = z_ref[...]
    h = jnp.sin(fr_ref[0:1, :] * (hdot(z, w1_ref[...]) + b1_ref[...]))
    h = jnp.sin(fr_ref[1:2, :] * (hdot(h, w2_ref[...]) + b2_ref[...]))
    o_ref[...] = hdot_nt(w3t_ref[...], h) * jnp.exp(-hdot_nt(dect_ref[...], z))


def _hy_filter(zfeat, w1, b1, w2, b2, freq, w3t, dect):
    l = zfeat.shape[0]
    tl = min(HYF_TL, l)
    nout = w3t.shape[0]
    return pl.pallas_call(
        _hyfilt_kernel,
        grid=(l // tl,),
        in_specs=[pl.BlockSpec((tl, LANES), lambda i: (i, 0))]
        + [_const_spec(a.shape) for a in (w1, b1, w2, b2, freq, w3t, dect)],
        out_specs=pl.BlockSpec((nout, tl), lambda i: (0, i)),
        out_shape=jax.ShapeDtypeStruct((nout, l), F32),
        compiler_params=_cparams(("parallel",), 32),
        name="hyena_filter",
    )(zfeat, w1, b1, w2, b2, freq, w3t, dect)


def _fft_tables(n1):
    n2 = FFT_N2
    n = n1 * n2
    k1 = np.arange(n1, dtype=np.float64)
    ang1 = 2.0 * np.pi * np.outer(k1, k1) / n1
    f1 = np.concatenate([np.cos(ang1), -np.sin(ang1)], axis=0)
    c3 = np.concatenate([np.cos(ang1), -np.sin(ang1)], axis=1)
    k2 = np.arange(n2, dtype=np.float64)
    ang2 = 2.0 * np.pi * np.outer(k2, k2) / n2
    c2, s2 = np.cos(ang2), np.sin(ang2)
    f2 = np.block([[c2, -s2], [s2, c2]])
    f2i = np.block([[c2, s2], [-s2, c2]])
    angt = 2.0 * np.pi * np.outer(k1, k2) / n
    bf = tuple(jnp.asarray(a, F32).astype(BF16) for a in (f1, c3, f2, f2i))
    return bf + (jnp.asarray(np.cos(angt), F32), jnp.asarray(-np.sin(angt), F32))


def _fft_forward(x3, f1, f2, tw_re, tw_im):
    cc = x3.shape[0]
    n1 = f1.shape[0] // 2
    xb = x3.astype(BF16)
    rows = []
    for c in range(0, cc, 2):
        a = _dot(f1, jnp.concatenate([xb[c], xb[c + 1]], axis=1))
        for a_re, a_im in ((a[:n1, :FFT_N2], a[n1:, :FFT_N2]), (a[:n1, FFT_N2:], a[n1:, FFT_N2:])):
            rows.append(jnp.concatenate([a_re * tw_re - a_im * tw_im, a_re * tw_im + a_im * tw_re], axis=1))
    return _dot(jnp.concatenate(rows, axis=0).astype(BF16), f2)


def _spectrum_kernel(g_ref, f1_ref, f2_ref, twre_ref, twim_ref, ore_ref, oim_ref):
    cc, n1, n2 = g_ref.shape
    x = _fft_forward(g_ref[...], f1_ref[...], f2_ref[...], twre_ref[...], twim_ref[...])
    ore_ref[...] = x[:, :n2].reshape(cc, n1, n2)
    oim_ref[...] = x[:, n2:].reshape(cc, n1, n2)


def _conv_kernel(u_ref, gate_ref, bias_ref, hre_ref, him_ref, f1_ref, c3_ref, f2_ref, f2i_ref, twre_ref, twim_ref,
                 o_ref):
    _, cc, nt1, n2 = u_ref.shape
    n1 = 2 * nt1
    tw_re, tw_im = twre_ref[...], twim_ref[...]
    u = u_ref[0]
    x = _fft_forward(u, f1_ref[...], f2_ref[...], tw_re, tw_im)
    x_re, x_im = x[:, :n2], x[:, n2:]
    h_re, h_im = hre_ref[...].reshape(cc * n1, n2), him_ref[...].reshape(cc * n1, n2)
    y = jnp.concatenate([x_re * h_re - x_im * h_im, x_re * h_im + x_im * h_re], axis=1).astype(BF16)
    cm = _dot(y, f2i_ref[...])
    c3 = c3_ref[...]
    scale = 1.0 / (n1 * n2)
    for c in range(0, cc, 2):
        d = []
        for ch in (c, c + 1):
            c_re, c_im = cm[ch * n1:(ch + 1) * n1, :n2], cm[ch * n1:(ch + 1) * n1, n2:]
            d.append(jnp.concatenate([c_re * tw_re + c_im * tw_im, c_im * tw_re - c_re * tw_im], axis=0))
        yt = _dot(c3, jnp.concatenate(d, axis=1).astype(BF16))
        for k, ch in enumerate((c, c + 1)):
            conv_out = yt[:, k * n2:(k + 1) * n2] * scale + bias_ref[ch] * u[ch].astype(F32)
            o_ref[0, ch] = (gate_ref[0, ch].astype(F32) * conv_out).astype(o_ref.dtype)


def _fft_spectrum(g3, tables):
    c, n1, n2 = g3.shape
    cc = min(FFT_CC, c)
    f1, _, f2, _, tw_re, tw_im = tables
    spec = pl.BlockSpec((cc, n1, n2), lambda i: (i, 0, 0))
    return pl.pallas_call(
        _spectrum_kernel,
        grid=(c // cc,),
        in_specs=[spec] + [_const_spec(a.shape) for a in (f1, f2, tw_re, tw_im)],
        out_specs=[spec, spec],
        out_shape=[jax.ShapeDtypeStruct(g3.shape, F32)] * 2,
        compiler_params=_cparams(("parallel",), 48),
        name="hyena_spectrum",
    )(g3, f1, f2, tw_re, tw_im)


def _fft_conv(u4, u_ch, gate4, gate_ch, bias3, hre, him, h_ch, tables):
    b, _, nt1, n2 = u4.shape
    n1 = 2 * nt1
    cc = FFT_CC
    f1, c3, f2, f2i, tw_re, tw_im = tables
    f1 = f1[:, :nt1]
    c3 = c3[:nt1]

    def seq_spec(ch0):
        return pl.BlockSpec((1, cc, nt1, n2), lambda ct, i: (i, ch0 // cc + ct, 0, 0))

    hspec = pl.BlockSpec((cc, n1, n2), lambda ct, i: (h_ch // cc + ct, 0, 0))
    return pl.pallas_call(
        _conv_kernel,
        grid=(BR_W // cc, b),
        in_specs=[seq_spec(u_ch), seq_spec(gate_ch), pl.BlockSpec((cc, 1, n2), lambda ct, i: (ct, 0, 0)), hspec, hspec]
        + [_const_spec(a.shape) for a in (f1, c3, f2, f2i, tw_re, tw_im)],
        out_specs=seq_spec(0),
        out_shape=jax.ShapeDtypeStruct((b, BR_W, nt1, n2), BF16),
        compiler_params=_cparams(("parallel", "parallel"), 48),
        name="hyena_conv",
    )(u4, gate4, bias3, hre, him, f1, c3, f2, f2i, tw_re, tw_im)


def _hyena_features(l):
    t = np.linspace(0.0, 1.0, l)[:, None]
    bands = np.linspace(1e-4, HY_BANDS - 1, HY_BANDS)[None, :]
    w = (2.0 * np.pi / l) * np.arange(l)[:, None]
    z = np.concatenate([t, np.cos(bands * w), -np.sin(bands * w)], axis=-1)
    return jnp.asarray(np.pad(z, ((0, 0), (0, LANES - HY_EMB))), F32)


def _hyena_branch(hy, lp, tables):
    b, _, l = hy.shape
    n2 = FFT_N2
    nt1 = l // n2
    pad = LANES - HY_FH
    w1 = jnp.pad(lp['hy_filt_w1'], ((0, LANES - HY_EMB), (0, pad)))
    w2 = jnp.pad(lp['hy_filt_w2'], ((0, pad), (0, pad)))
    w3t = jnp.pad(lp['hy_filt_w3'], ((0, pad), (0, 0))).T
    b1 = jnp.pad(lp['hy_filt_b1'], (0, pad)).reshape(1, LANES)
    b2 = jnp.pad(lp['hy_filt_b2'], (0, pad)).reshape(1, LANES)
    freq = jnp.pad(lp['hy_filt_freq'], ((0, 0), (0, pad)))
    dect = jnp.pad(jnp.abs(lp['hy_decay']).reshape(-1, 1), ((0, 0), (0, LANES - 1)))
    filt = _hy_filter(_hyena_features(l), w1, b1, w2, b2, freq, w3t, dect).reshape(2, 2, HY_C, l)
    hf, hb = filt[:, 0], filt[:, 1]
    g = jnp.concatenate([hf, jnp.zeros((2, HY_C, 1), F32), jnp.flip(hb, axis=-1)[..., :l - 1]], axis=-1)
    hre, him = _fft_spectrum(g.reshape(2 * HY_C, 2 * nt1, n2), tables)
    hy4 = hy.reshape(b, 3 * HY_C, nt1, n2)
    bias3 = jnp.broadcast_to(lp['hy_bias'][:, :, None, None], (2, HY_C, 1, n2))
    z = _fft_conv(hy4, 0, hy4, HY_C, bias3[0], hre, him, 0, tables)
    y = _fft_conv(z, 0, hy4, 2 * HY_C, bias3[1], hre, him, HY_C, tables)
    return y.reshape(b, HY_C, l)


def _mixer(x, lp, tables):
    w = lp['w_in']
    zpad = jnp.zeros((D_MODEL, LANES - 2 * GLA_R - 2 * M2_H), w.dtype)
    w_p = jnp.concatenate(
        [w[:, O_XBC:O_DT], w[:, O_M2Z:O_XBC], w[:, O_GA:O_GG], w[:, O_DT:O_GQ], zpad,
         w[:, O_GQ:O_GV], w[:, O_GV:O_GA], w[:, O_GG:O_HQ], w[:, O_HQ:O_HF], w[:, O_HF:O_HI], w[:, O_HI:O_HG],
         w[:, O_HG:O_GATE]], axis=1).astype(BF16)
    w_hyt = w[:, O_HY:O_M2Z].T.astype(BF16)
    conv_hyt = jnp.concatenate([lp['hy_conv_w'], lp['hy_conv_b'][None, :]], axis=0).T
    hy, xbc, m2z, small, gqk, gv, gg, hq, hf, hi, hgg = _inproj(
        x, lp['mix_norm_pre'], w_hyt, conv_hyt, w_p, lp['m2_conv_w'], lp['m2_conv_b'].reshape(1, N_XBC))

    y_a = _hyena_branch(hy, lp, tables)

    dt_lanes = slice(SM_DT, SM_DT + 2 * M2_H)
    dtb_row = jnp.zeros((1, LANES), F32).at[0, dt_lanes].set(lp['m2_dt_bias'])
    a_row = jnp.zeros((1, LANES), F32).at[0, dt_lanes].set(-jnp.exp(lp['m2_A_log'].astype(F32)))
    dskip_row = jnp.repeat(lp['m2_D'], M2_P).reshape(1, BR_W)
    y_b = _ssd_branch(xbc, small, m2z, dtb_row, a_row, dskip_row, lp['m2_norm'].reshape(1, BR_W))

    wg2 = jnp.zeros((2, LANES, GLA_H * GLA_DK), F32)
    for dr in range(2):
        wg2 = wg2.at[dr, SM_GA + dr * GLA_R:SM_GA + (dr + 1) * GLA_R].set(lp['gla_w_gate2'][dr])
    y_c = _gla_branch(gqk, gv, small, gg, wg2, lp['gla_b_gate'].reshape(2, 1, GLA_H * GLA_DK),
                      jnp.tile(lp['gla_norm'], GLA_H).reshape(1, BR_W))

    y_d = _hg_branch(hq, hf, hi, hgg, lp['hg_lb'].reshape(2, 1, HG_H * HG_DK),
                     jnp.tile(lp['hg_norm'], HG_H).reshape(1, BR_W))

    return _merge(x, (y_a, y_b, y_c, y_d), lp['mix_norm_pre'], lp['mix_norm_post'],
                  w[:, O_GATE:].astype(BF16), lp['w_branch'].astype(BF16), lp['w_out'].astype(BF16))


def kernel(x_prompt, x_sample, ffn1_norm_pre, ffn1_norm_post, ffn1_w_gu, ffn1_w_down, mix_norm_pre, mix_norm_post, w_in, hy_conv_w, hy_conv_b, hy_filt_w1, hy_filt_b1, hy_filt_w2, hy_filt_b2, hy_filt_freq, hy_filt_w3, hy_decay, hy_bias, m2_conv_w, m2_conv_b, m2_dt_bias, m2_A_log, m2_D, m2_norm, gla_w_gate2, gla_b_gate, gla_norm, hg_lb_param, hg_norm, w_branch, w_out, ffn2_norm_pre, ffn2_norm_post, ffn2_w_gu, ffn2_w_down):
    sm = jax.nn.softmax(hg_lb_param.astype(F32), axis=0)
    hg_lb = jnp.cumsum(sm, axis=0) - sm[0]
    params = {
        'mix_norm_pre': mix_norm_pre, 'mix_norm_post': mix_norm_post, 'w_in': w_in,
        'hy_conv_w': hy_conv_w, 'hy_conv_b': hy_conv_b,
        'hy_filt_w1': hy_filt_w1, 'hy_filt_b1': hy_filt_b1, 'hy_filt_w2': hy_filt_w2, 'hy_filt_b2': hy_filt_b2,
        'hy_filt_freq': hy_filt_freq, 'hy_filt_w3': hy_filt_w3, 'hy_decay': hy_decay, 'hy_bias': hy_bias,
        'm2_conv_w': m2_conv_w, 'm2_conv_b': m2_conv_b, 'm2_dt_bias': m2_dt_bias, 'm2_A_log': m2_A_log,
        'm2_D': m2_D, 'm2_norm': m2_norm,
        'gla_w_gate2': gla_w_gate2, 'gla_b_gate': gla_b_gate, 'gla_norm': gla_norm,
        'hg_lb': hg_lb, 'hg_norm': hg_norm, 'w_branch': w_branch, 'w_out': w_out,
    }
    nb = x_prompt.shape[0]
    x = jnp.concatenate([x_prompt, x_sample], axis=0)
    depth = w_in.shape[0]
    tables = _fft_tables(2 * x.shape[1] // FFT_N2)
    for layer in range(depth):
        lp = {name: arr[layer] for name, arr in params.items()}
        x = _ffn(x, ffn1_norm_pre[layer], ffn1_norm_post[layer],
                 ffn1_w_gu[layer].astype(BF16), ffn1_w_down[layer].astype(BF16))
        x = _mixer(x, lp, tables)
        x = _ffn(x, ffn2_norm_pre[layer], ffn2_norm_post[layer],
                 ffn2_w_gu[layer].astype(BF16), ffn2_w_down[layer].astype(BF16))
    return (x[:nb], x[nb:])
```

```python
import functools
import math

import numpy as np
import jax
import jax.numpy as jnp
from jax import lax
from jax.experimental import pallas as pl
from jax.experimental.pallas import tpu as pltpu

F32 = jnp.float32
BF16 = jnp.bfloat16
EPS = 1e-6
HIGHEST = lax.Precision.HIGHEST
LOG2E = 1.4426950408889634

SUBLANES = 8
LANES = 128
VMEM_BYTES = 64 * 1024 * 1024

D_MODEL = 1024
BR_W = 256
D_FF = 2816
HY_C = 256
HY_BANDS = 16
HY_EMB = 33
HY_FH = 64
M2_H = 4
M2_P = 64
M2_N = 64
GLA_H = 4
GLA_DK = 32
GLA_DV = 64
GLA_R = 16
GLA_GATE_NORM = 16.0
HG_H = 4
HG_DK = 64
HG_DV = 64
N_GATE = 4 * D_MODEL
O_HY, O_M2Z, O_XBC, O_DT, O_GQ, O_GK, O_GV, O_GA, O_GG, O_HQ, O_HF, O_HI, O_HG, O_GATE = (
    0, 768, 1024, 1536, 1544, 1672, 1800, 2056, 2088, 2344, 2600, 3112, 3368, 3624)
N_CONV = 768 + 512
SM_GA = 0
SM_DT = 32


def _cparams(sem, vmem_mb):
    return pltpu.CompilerParams(dimension_semantics=sem, vmem_limit_bytes=vmem_mb * 1024 * 1024)


def _dot(a, b):
    return jnp.dot(a, b, preferred_element_type=F32)


def _dot_nt(a, b):
    return lax.dot_general(a, b, (((1,), (1,)), ((), ())), preferred_element_type=F32)


def _dot_tn(a, b):
    return lax.dot_general(a, b, (((0,), (0,)), ((), ())), preferred_element_type=F32)


def _rms(x, w):
    return x * lax.rsqrt(jnp.mean(x * x, axis=-1, keepdims=True) + EPS) * w


def _silu(x):
    return x * jax.nn.sigmoid(x)


def _const_spec(shape):
    nd = len(shape)
    return pl.BlockSpec(shape, lambda *_: (0,) * nd, pipeline_mode=pl.Buffered(1))


FFN_TM = 512
FFN_FC = 256


def _ffn_kernel(*refs, n_first, two_in, two_out):
    refs = list(refs)
    xa_ref = refs.pop(0)
    xb_ref = refs.pop(0) if two_in else None
    npre_ref, npost_ref, wgu_ref, wd_ref = refs[:4]
    outs = refs[4:]
    in_first = pl.program_id(0) < n_first if (two_in or two_out) else None
    x = jnp.where(in_first, xa_ref[0], xb_ref[0]) if two_in else xa_ref[0]
    xn = _rms(x, npre_ref[...]).astype(BF16)
    acc = jnp.zeros(x.shape, F32)
    for j in range(D_FF // FFN_FC):
        g = _dot(xn, wgu_ref[:, j * FFN_FC:(j + 1) * FFN_FC])
        u = _dot(xn, wgu_ref[:, D_FF + j * FFN_FC:D_FF + (j + 1) * FFN_FC])
        h = (_silu(g) * u).astype(BF16)
        acc = acc + _dot(h, wd_ref[j * FFN_FC:(j + 1) * FFN_FC, :])
    out = x + 0.5 * _rms(acc, npost_ref[...])
    if two_out:
        @pl.when(in_first)
        def _():
            outs[0][0] = out

        @pl.when(jnp.logical_not(in_first))
        def _():
            outs[1][0] = out
    else:
        outs[0][0] = out


def _ffn(xs, npre, npost, wgu, wd, n_first=None, two_out=False):
    two_in = isinstance(xs, tuple)
    if two_in:
        n_first = xs[0].shape[0]
        b = n_first + xs[1].shape[0]
    else:
        xs = (xs,)
        b = xs[0].shape[0]
    _, l, d = xs[0].shape
    tm = min(FFN_TM, l)
    nj = l // tm
    xspec = pl.BlockSpec((1, tm, d), lambda i, j: (i, j, 0))
    first_spec = pl.BlockSpec((1, tm, d), lambda i, j: (jnp.minimum(i, n_first - 1), jnp.where(i < n_first, j, nj - 1), 0))
    second_spec = pl.BlockSpec((1, tm, d), lambda i, j: (jnp.maximum(i - n_first, 0), jnp.where(i < n_first, 0, j), 0))
    if two_out:
        out_specs = [first_spec, second_spec]
        out_shape = [jax.ShapeDtypeStruct((n_first, l, d), F32), jax.ShapeDtypeStruct((b - n_first, l, d), F32)]
    else:
        out_specs, out_shape = xspec, jax.ShapeDtypeStruct((b, l, d), F32)
    return pl.pallas_call(
        functools.partial(_ffn_kernel, n_first=n_first, two_in=two_in, two_out=two_out),
        grid=(b, nj),
        in_specs=([first_spec, second_spec] if two_in else [xspec])
        + [_const_spec((1, d)), _const_spec((1, d)), _const_spec(wgu.shape), _const_spec(wd.shape)],
        out_specs=out_specs,
        out_shape=out_shape,
        compiler_params=_cparams(("arbitrary", "arbitrary") if two_out else ("parallel", "parallel"), 48),
        name="ffn",
    )(*xs, npre.reshape(1, d), npost.reshape(1, d), wgu, wd)


INP_TM = 512
_INP_OUTS = (("m2z", 256, BF16), ("small", 128, F32), ("gqk", 256, BF16), ("gv", 256, BF16), ("gg", 256, BF16),
             ("hq", 256, BF16), ("hf", 512, F32), ("hi", 256, BF16), ("hgg", 256, BF16))
N_HY = 3 * HY_C
N_XBC = BR_W + 4 * M2_N
N_INP = N_XBC + sum(w for _, w, _ in _INP_OUTS)


def _inproj_kernel(x_ref, xp_ref, xn_ref, nw_ref, wt_ref, cwt_ref, w_ref, cw_ref, cb_ref, hy_ref, xbc_ref, *rest):
    i = pl.program_id(1)
    n = pl.num_programs(1)
    nw = nw_ref[...]
    h = _rms(x_ref[0], nw).astype(BF16)
    tm = h.shape[0]
    hp = _rms(xp_ref[0], nw).astype(BF16)
    hn = _rms(xn_ref[0], nw).astype(BF16)
    first, last = i == 0, i == n - 1

    wt = wt_ref[...]
    p = _dot_nt(wt, h)
    pp = jnp.where(first, 0.0, _dot_nt(wt, hp)[:, SUBLANES - 1:SUBLANES])
    pn = jnp.where(last, 0.0, _dot_nt(wt, hn)[:, 0:1])
    lane = lax.broadcasted_iota(jnp.int32, (1, tm), 1)
    prev = jnp.where(lane == 0, pp, pltpu.roll(p, 1, 1))
    nxt = jnp.where(lane == tm - 1, pn, pltpu.roll(p, tm - 1, 1))
    cwt = cwt_ref[...]
    hy_ref[0] = (cwt[:, 3:4] + prev * cwt[:, 0:1] + p * cwt[:, 1:2] + nxt * cwt[:, 2:3]).astype(hy_ref.dtype)

    wc = w_ref[:, :N_XBC]
    p = _dot(h, wc)
    pp = jnp.where(first, 0.0, _dot(hp, wc)[SUBLANES - 1:SUBLANES])
    pn = jnp.where(last, 0.0, _dot(hn, wc)[0:1])
    row = lax.broadcasted_iota(jnp.int32, (tm, 1), 0)
    prev = jnp.where(row == 0, pp, pltpu.roll(p, 1, 0))
    nxt = jnp.where(row == tm - 1, pn, pltpu.roll(p, tm - 1, 0))
    cw = cw_ref[...]
    xbc_ref[0] = _silu(cb_ref[...] + prev * cw[0:1] + p * cw[1:2] + nxt * cw[2:3]).astype(xbc_ref.dtype)
    off = N_XBC
    for (_, width, _), ref in zip(_INP_OUTS, rest):
        ref[0] = _dot(h, w_ref[:, off:off + width]).astype(ref.dtype)
        off += width


def _inproj(x, nw, w_hyt, conv_hyt, w_p, conv_w, conv_b):
    b, l, d = x.shape
    tm = min(INP_TM, l)
    r8 = tm // SUBLANES
    nb8 = l // SUBLANES
    outs = (("xbc", N_XBC, BF16),) + _INP_OUTS
    return pl.pallas_call(
        _inproj_kernel,
        grid=(b, l // tm),
        in_specs=[pl.BlockSpec((1, tm, d), lambda i, j: (i, j, 0)),
                  pl.BlockSpec((1, SUBLANES, d), lambda i, j: (i, jnp.maximum(j * r8 - 1, 0), 0)),
                  pl.BlockSpec((1, SUBLANES, d), lambda i, j: (i, jnp.minimum((j + 1) * r8, nb8 - 1), 0)),
                  _const_spec((1, d)), _const_spec(w_hyt.shape), _const_spec(conv_hyt.shape), _const_spec(w_p.shape),
                  _const_spec(conv_w.shape), _const_spec(conv_b.shape)],
        out_specs=[pl.BlockSpec((1, N_HY, tm), lambda i, j: (i, 0, j))]
        + [pl.BlockSpec((1, tm, w), lambda i, j: (i, j, 0)) for _, w, _ in outs],
        out_shape=[jax.ShapeDtypeStruct((b, N_HY, l), BF16)]
        + [jax.ShapeDtypeStruct((b, l, w), dt) for _, w, dt in outs],
        compiler_params=_cparams(("parallel", "parallel"), 48),
        name="inproj",
    )(x, x, x, nw.reshape(1, d), w_hyt, conv_hyt, w_p, conv_w, conv_b)


MRG_TM = 512


def _merge_kernel(x_ref, ya_ref, yb_ref, yc_ref, yd_ref, npre_ref, npost_ref, wg_ref, wb_ref, wo_ref, o_ref):
    x = x_ref[0]
    h = _rms(x, npre_ref[...]).astype(BF16)
    merged = jnp.zeros(x.shape, F32)
    for n, y_ref in enumerate((ya_ref, yb_ref, yc_ref, yd_ref)):
        gate = jax.nn.sigmoid(_dot(h, wg_ref[:, n * D_MODEL:(n + 1) * D_MODEL]))
        y = y_ref[0].astype(BF16)
        proj = _dot_tn(y, wb_ref[n]) if n == 0 else _dot(y, wb_ref[n])
        merged = merged + gate * proj
    out = _dot(merged.astype(BF16), wo_ref[...])
    o_ref[0] = x + _rms(out, npost_ref[...])


def _merge(x, ys, npre, npost, w_gate, w_branch, w_out):
    b, l, d = x.shape
    tm = min(MRG_TM, l)
    xspec = pl.BlockSpec((1, tm, d), lambda i, j: (i, j, 0))
    yspec = pl.BlockSpec((1, tm, BR_W), lambda i, j: (i, j, 0))
    return pl.pallas_call(
        _merge_kernel,
        grid=(b, l // tm),
        in_specs=[xspec, pl.BlockSpec((1, BR_W, tm), lambda i, j: (i, 0, j)), yspec, yspec, yspec,
                  _const_spec((1, d)), _const_spec((1, d)),
                  _const_spec(w_gate.shape), _const_spec(w_branch.shape), _const_spec(w_out.shape)],
        out_specs=xspec,
        out_shape=jax.ShapeDtypeStruct(x.shape, F32),
        compiler_params=_cparams(("parallel", "parallel"), 48),
        name="merge",
    )(x, *ys, npre.reshape(1, d), npost.reshape(1, d), w_gate, w_branch, w_out)


def _split_dot_left(m_bf16, x, passes):
    acc = None
    r = x
    for p in range(passes):
        piece = r.astype(BF16)
        t = _dot(m_bf16, piece)
        acc = t if acc is None else acc + t
        if p + 1 < passes:
            r = r - piece.astype(F32)
    return acc


def _split_dot_right(x, m_bf16, passes):
    acc = None
    r = x
    for p in range(passes):
        piece = r.astype(BF16)
        t = _dot(piece, m_bf16)
        acc = t if acc is None else acc + t
        if p + 1 < passes:
            r = r - piece.astype(F32)
    return acc


def _log_sigmoid(x):
    return jnp.minimum(x, 0.0) - jnp.log1p(jnp.exp(-jnp.abs(x)))


def _softplus(x):
    return jnp.maximum(x, 0.0) + jnp.log1p(jnp.exp(-jnp.abs(x)))


def _ones_where(mask):
    return jnp.where(mask, 1.0, 0.0).astype(BF16)


def _head_norm(o, nw_row, dv):
    cv = o.shape[1]
    r = lax.broadcasted_iota(jnp.int32, (cv, cv), 0) // dv
    s = lax.broadcasted_iota(jnp.int32, (cv, cv), 1) // dv
    ms = _split_dot_right(o * o, _ones_where(r == s), 2) * (1.0 / dv)
    return o * lax.rsqrt(ms + EPS) * nw_row


SCAN_TB = 128
SCAN_NSEQ = 2
GLA_DIAG = 8
NEG_BIG = -1e30


def _gla_levels(tb):
    out, m = [], GLA_DIAG
    while m < tb:
        out.append(m)
        m *= 2
    return out


def _gla_consts(tb, ck, dk, rev):
    c = GLA_DIAG
    nh = ck // dk
    t = np.arange(tb)
    tau = tb - 1 - t if rev else t
    tt, rr = tau[:, None], tau[None, :]
    col = np.arange(LANES)[None, :]
    ecat = col == (np.tile(np.arange(ck) // dk, c) * c + np.repeat(np.arange(c), ck))[:, None]
    spread = np.arange(LANES)[:, None] == (np.repeat(np.arange(nh), tb) * c + np.tile(tau % c, nh))[None, :]
    dmask = np.tile((tt // c == rr // c) & (rr <= tt), (1, nh))
    return jnp.asarray(rr <= tt, BF16), jnp.asarray(ecat, BF16), jnp.asarray(spread, BF16), jnp.asarray(dmask, F32)


def _gla_block(q, k, v, la, st, tril_ref, ecat_ref, spread_ref, dmask_ref, *, rev, dk, dv):
    tb, ck = q.shape
    cv = v.shape[1]
    c = GLA_DIAG
    nh = ck // dk
    levels = _gla_levels(tb)

    def rows(a, b):
        return slice(tb - b, tb - a) if rev else slice(a, b)

    def row(x):
        return rows(x, x + 1)

    cb = _split_dot_left(tril_ref[...], la, 3)
    total = cb[row(tb - 1)]
    lane_hk = lax.broadcasted_iota(jnp.int32, (1, ck), 1) // dk
    lane_hv = lax.broadcasted_iota(jnp.int32, (1, cv), 1) // dv
    vh = [jnp.where(lane_hv == h, v, 0.0) for h in range(nh)]

    qe = (q * jnp.exp(cb)).astype(BF16)
    ke = (k * jnp.exp(total - cb)).astype(BF16)
    o = _dot_nt(qe, st.astype(BF16))
    hmask = (lax.broadcasted_iota(jnp.int32, (cv, ck), 0) // dv) == (lax.broadcasted_iota(jnp.int32, (cv, ck), 1) // dk)
    st_new = st * jnp.exp(total) + jnp.where(hmask, _dot_tn(v.astype(BF16), ke), 0.0)

    g3 = (tb // c, c, ck)
    q3, k3, cb3 = q.reshape(g3), k.reshape(g3), (cb * LOG2E).reshape(g3)
    ps = []
    for j in range(c):
        jt = c - 1 - j if rev else j
        e = jnp.exp2(jnp.minimum(cb3 - jnp.broadcast_to(cb3[:, jt:jt + 1, :], g3), 0.0))
        ps.append((q3 * jnp.broadcast_to(k3[:, jt:jt + 1, :], g3) * e).reshape(tb, ck).astype(BF16))
    att8 = _dot(jnp.concatenate(ps, axis=1), ecat_ref[...]).astype(BF16)
    att = _dot(att8, spread_ref[...]) * dmask_ref[...]
    o = o + _dot(att.astype(BF16), jnp.concatenate(vh, axis=0).astype(BF16))

    tau = lax.broadcasted_iota(jnp.int32, (tb, 1), 0)
    if rev:
        tau = tb - 1 - tau
    scores = []
    for li, m in enumerate(levels):
        early = (tau % (2 * m)) < m
        npairs = tb // (2 * m)
        bnd = [None] * npairs
        for p in range(npairs):
            bnd[npairs - 1 - p if rev else p] = jnp.broadcast_to(cb[row(p * 2 * m + m - 1)], (2 * m, ck))
        diff = cb - jnp.concatenate(bnd, axis=0)
        w = jnp.exp(jnp.where(early, -diff, diff)) * jnp.where(early, k, q)
        kh = [jnp.where(lane_hk == h, w, 0.0) for h in range(nh)]
        npad = (-nh * m) % LANES
        for p in range(tb // (2 * m)):
            er, lr = rows(p * 2 * m, p * 2 * m + m), rows(p * 2 * m + m, (p + 1) * 2 * m)
            kst = [kh[h][er] for h in range(nh)] + ([jnp.zeros((npad, ck), F32)] if npad else [])
            a = _dot_nt(w[lr].astype(BF16), jnp.concatenate(kst, axis=0).astype(BF16))
            scores.append((m, p, er, a.astype(BF16)))
    for m in levels:
        npad = (-nh * m) % LANES
        parts = [None] * (tb // m)
        for mm, p, er, a in scores:
            if mm != m:
                continue
            vst = [vh[h][er] for h in range(nh)] + ([jnp.zeros((npad, cv), F32)] if npad else [])
            early_idx, late_idx = (2 * p, 2 * p + 1)
            if rev:
                early_idx, late_idx = tb // m - 1 - early_idx, tb // m - 1 - late_idx
            parts[late_idx] = _dot(a, jnp.concatenate(vst, axis=0).astype(BF16))
            parts[early_idx] = jnp.zeros((m, cv), F32)
        o = o + jnp.concatenate(parts, axis=0)
    return o, st_new


def _reset_state(st_ref):
    @pl.when(pl.program_id(1) == 0)
    def _():
        st_ref[...] = jnp.zeros(st_ref.shape, F32)


N_SCAN_CONSTS = 4


def _gla_kernel(*refs, rev, final):
    qk_ref, v_ref, sm_ref = refs[:3]
    consts = refs[3:3 + N_SCAN_CONSTS]
    if final:
        wgh_ref, wgl_ref, bg_ref, oprev_ref, g_ref, nw_ref, o_ref, st_ref = refs[3 + N_SCAN_CONSTS:]
    else:
        wgh_ref, wgl_ref, bg_ref, o_ref, st_ref = refs[3 + N_SCAN_CONSTS:]
    _reset_state(st_ref)
    ck = GLA_H * GLA_DK
    for s in range(st_ref.shape[0]):
        qk = qk_ref[s].astype(F32)
        q = qk[:, :ck] * (GLA_DK ** -0.5)
        k = qk[:, ck:]
        v = v_ref[s].astype(F32)
        a = sm_ref[s]
        a_hi = a.astype(BF16)
        a_lo = (a - a_hi.astype(F32)).astype(BF16)
        pre = _dot(a_hi, wgh_ref[...]) + _dot(a_lo, wgh_ref[...]) + _dot(a_hi, wgl_ref[...]) + bg_ref[...]
        la = _log_sigmoid(pre) * (1.0 / GLA_GATE_NORM)
        o, st_ref[s] = _gla_block(q, k, v, la, st_ref[s], *consts, rev=rev, dk=GLA_DK, dv=GLA_DV)
        if final:
            o = o + oprev_ref[s]
            o_ref[s] = (_head_norm(o, nw_ref[...], GLA_DV) * _silu(g_ref[s].astype(F32))).astype(o_ref.dtype)
        else:
            o_ref[s] = o


def _hg_kernel(*refs, rev, final):
    q_ref, f_ref, i_ref = refs[:3]
    consts = refs[3:3 + N_SCAN_CONSTS]
    if final:
        lb_ref, oprev_ref, g_ref, nw_ref, o_ref, st_ref = refs[3 + N_SCAN_CONSTS:]
    else:
        lb_ref, o_ref, st_ref = refs[3 + N_SCAN_CONSTS:]
    _reset_state(st_ref)
    lb = lb_ref[...]
    for s in range(st_ref.shape[0]):
        q = q_ref[s].astype(F32) * (HG_DK ** -0.5)
        z = f_ref[s]
        pos = z >= 0.0
        e = jnp.exp(-jnp.abs(z))
        t = 1.0 + e
        r = 1.0 / t
        la = jnp.minimum(z, 0.0) - jnp.log(t) + jnp.log(1.0 + lb * jnp.where(pos, e, 1.0 / e))
        k = (1.0 - lb) * jnp.where(pos, e * r, r)
        v = i_ref[s].astype(F32)
        o, st_ref[s] = _gla_block(q, k, v, la, st_ref[s], *consts, rev=rev, dk=HG_DK, dv=HG_DV)
        if final:
            o = o + oprev_ref[s]
            o_ref[s] = (_head_norm(o, nw_ref[...], HG_DV) * _silu(g_ref[s].astype(F32))).astype(o_ref.dtype)
        else:
            o_ref[s] = o


def _scan_call(body, name, seq_ins, const_ins, fin_seq, fin_const, scratch, b, l, rev):
    tb = min(SCAN_TB, l)
    nblk = l // tb
    final = fin_seq is not None
    nseq = SCAN_NSEQ if b % SCAN_NSEQ == 0 else 1

    def seq_spec(width, lane_idx):
        if rev:
            return pl.BlockSpec((nseq, tb, width), lambda i, j: (i, nblk - 1 - j, lane_idx))
        return pl.BlockSpec((nseq, tb, width), lambda i, j: (i, j, lane_idx))

    arrays = [a for a, _, _ in seq_ins] + list(const_ins)
    specs = [seq_spec(w, li) for _, w, li in seq_ins] + [_const_spec(a.shape) for a in const_ins]
    if final:
        arrays += [a for a, _, _ in fin_seq] + list(fin_const)
        specs += [seq_spec(w, li) for _, w, li in fin_seq] + [_const_spec(a.shape) for a in fin_const]
    return pl.pallas_call(
        functools.partial(body, rev=rev, final=final),
        grid=(b // nseq, nblk),
        in_specs=specs,
        out_specs=seq_spec(BR_W, 0),
        out_shape=jax.ShapeDtypeStruct((b, l, BR_W), BF16 if final else F32),
        scratch_shapes=[pltpu.VMEM((nseq,) + tuple(scratch), F32)],
        compiler_params=_cparams(("parallel", "arbitrary"), 32),
        name=name,
    )(*arrays)


def _gla_branch(gqk, gv, small, gg, wg2, bg, nw):
    b, l, _ = gqk.shape
    tb = min(SCAN_TB, l)
    ck = GLA_H * GLA_DK
    seq = [(gqk, 256, 0), (gv, 256, 0), (small, 128, 0)]
    scratch = (BR_W, ck)
    wg_hi = wg2.astype(BF16)
    wg_lo = (wg2 - wg_hi.astype(F32)).astype(BF16)
    o_f = _scan_call(_gla_kernel, "gla_fwd", seq, [*_gla_consts(tb, ck, GLA_DK, False), wg_hi[0], wg_lo[0], bg[0]],
                     None, None, scratch, b, l, False)
    return _scan_call(_gla_kernel, "gla_bwd", seq, [*_gla_consts(tb, ck, GLA_DK, True), wg_hi[1], wg_lo[1], bg[1]],
                      [(o_f, 256, 0), (gg, 256, 0)], [nw], scratch, b, l, True)


def _hg_branch(hq, hf, hi, hgg, lb, nw):
    b, l, _ = hq.shape
    tb = min(SCAN_TB, l)
    ck = HG_H * HG_DK
    scratch = (BR_W, ck)
    o_f = _scan_call(_hg_kernel, "hgrn_fwd", [(hq, 256, 0), (hf, 256, 0), (hi, 256, 0)],
                     [*_gla_consts(tb, ck, HG_DK, False), lb[0]], None, None, scratch, b, l, False)
    return _scan_call(_hg_kernel, "hgrn_bwd", [(hq, 256, 0), (hf, 256, 1), (hi, 256, 0)],
                      [*_gla_consts(tb, ck, HG_DK, True), lb[1]],
                      [(o_f, 256, 0), (hgg, 256, 0)], [nw], scratch, b, l, True)


def _ssd_kernel(*refs, rev, final):
    if final:
        xbc_ref, sm_ref, dtb_ref, a_ref, oprev_ref, z_ref, dskip_ref, nw_ref, o_ref, st_ref = refs
    else:
        xbc_ref, sm_ref, dtb_ref, a_ref, o_ref, st_ref = refs
    _reset_state(st_ref)
    for s in range(st_ref.shape[0]):
        y, x, st_ref[s] = _ssd_block(xbc_ref[s].astype(F32), sm_ref[s], dtb_ref[...], a_ref[...], st_ref[s], rev)
        if final:
            y = y + oprev_ref[s] + dskip_ref[...] * x
            o_ref[s] = _rms(y * _silu(z_ref[s].astype(F32)), nw_ref[...]).astype(o_ref.dtype)
        else:
            o_ref[s] = y


def _ssd_block(xbc, sm, dtb, a_row, st, rev):
    tb = xbc.shape[0]
    x = xbc[:, :BR_W]
    bm = xbc[:, BR_W:BR_W + 2 * M2_N].astype(BF16)
    cm = xbc[:, BR_W + 2 * M2_N:]
    dt = _softplus(sm + dtb)
    da = dt * a_row
    lane0 = SM_DT + (M2_H if rev else 0)

    t_i = lax.broadcasted_iota(jnp.int32, (tb, tb), 0)
    s_i = lax.broadcasted_iota(jnp.int32, (tb, tb), 1)
    if rev:
        incl, strict = s_i >= t_i, t_i < s_i
    else:
        incl, strict = s_i <= t_i, t_i > s_i
    m_incl = _ones_where(incl)
    edge = 0 if rev else tb - 1
    cum = _split_dot_left(m_incl, da, 3)
    suf = cum[edge:edge + 1, :] - cum
    expand = _ones_where((lax.broadcasted_iota(jnp.int32, (LANES, BR_W), 0) - lane0)
                         == (lax.broadcasted_iota(jnp.int32, (LANES, BR_W), 1) // M2_P))
    dt_x = _split_dot_right(dt, expand, 2)
    ecum_x = _split_dot_right(jnp.exp(cum), expand, 2)
    w_x = _split_dot_right(dt * jnp.exp(suf), expand, 2)
    xdt = (x * dt_x).astype(BF16)
    xw = (x * w_x).astype(BF16)

    lane_g = lax.broadcasted_iota(jnp.int32, (1, 2 * M2_N), 1) // M2_N
    lane_h = lax.broadcasted_iota(jnp.int32, (1, BR_W), 1) // M2_P
    gram = [_dot_nt(jnp.where(lane_g == g, cm, 0.0).astype(BF16), bm) for g in range(2)]
    y = jnp.zeros((tb, BR_W), F32)
    cols = [jnp.where(strict, jnp.broadcast_to(da[:, lane0 + h:lane0 + h + 1], (tb, tb)), 0.0) for h in range(M2_H)]
    e_all = _split_dot_left(m_incl, jnp.concatenate(cols, axis=1), 3)
    for h in range(M2_H):
        seg = jnp.exp(jnp.where(incl, e_all[:, h * tb:(h + 1) * tb], NEG_BIG))
        scores = (gram[h // (M2_H // 2)] * seg).astype(BF16)
        y = y + jnp.where(lane_h == h, _dot(scores, xdt), 0.0)
    y = y + ecum_x * _dot(cm.astype(BF16), st.astype(BF16))
    gmask = ((lax.broadcasted_iota(jnp.int32, st.shape, 0) // M2_N)
             == (lax.broadcasted_iota(jnp.int32, st.shape, 1) // (2 * M2_P)))
    return y, x, st * ecum_x[edge:edge + 1, :] + jnp.where(gmask, _dot_tn(bm, xw), 0.0)


def _ssd_branch(xbc, small, m2z, dtb_row, a_row, dskip_row, nw):
    b, l, _ = xbc.shape
    seq = [(xbc, 512, 0), (small, 128, 0)]
    shape = (2 * M2_N, BR_W)
    o_f = _scan_call(_ssd_kernel, "ssd_fwd", seq, [dtb_row, a_row], None, None, shape, b, l, False)
    return _scan_call(_ssd_kernel, "ssd_bwd", seq, [dtb_row, a_row], [(o_f, 256, 0), (m2z, 256, 0)],
                      [dskip_row, nw], shape, b, l, True)


FFT_N2 = 128
FFT_CC = 16
HYF_TL = 512


def _hyfilt_kernel(z_ref, w1_ref, b1_ref, w2_ref, b2_ref, fr_ref, w3t_ref, dect_ref, o_ref):
    def hdot(a, b):
        return jnp.dot(a, b, preferred_element_type=F32, precision=HIGHEST)

    def hdot_nt(a, b):
        return lax.dot_general(a, b, (((1,), (1,)), ((), ())), preferred_element_type=F32, precision=HIGHEST)

    z = z_ref[...]
    h = jnp.sin(fr_ref[0:1, :] * (hdot(z, w1_ref[...]) + b1_ref[...]))
    h = jnp.sin(fr_ref[1:2, :] * (hdot(h, w2_ref[...]) + b2_ref[...]))
    o_ref[...] = hdot_nt(w3t_ref[...], h) * jnp.exp(-hdot_nt(dect_ref[...], z))


def _hy_filter(zfeat, w1, b1, w2, b2, freq, w3t, dect):
    l = zfeat.shape[0]
    tl = min(HYF_TL, l)
    nout = w3t.shape[0]
    return pl.pallas_call(
        _hyfilt_kernel,
        grid=(l // tl,),
        in_specs=[pl.BlockSpec((tl, LANES), lambda i: (i, 0))]
        + [_const_spec(a.shape) for a in (w1, b1, w2, b2, freq, w3t, dect)],
        out_specs=pl.BlockSpec((nout, tl), lambda i: (0, i)),
        out_shape=jax.ShapeDtypeStruct((nout, l), F32),
        compiler_params=_cparams(("parallel",), 32),
        name="hyena_filter",
    )(zfeat, w1, b1, w2, b2, freq, w3t, dect)


def _fft_tables(n1):
    n2 = FFT_N2
    n = n1 * n2
    k1 = np.arange(n1, dtype=np.float64)
    ang1 = 2.0 * np.pi * np.outer(k1, k1) / n1
    f1 = np.concatenate([np.cos(ang1), -np.sin(ang1)], axis=0)
    c3 = np.concatenate([np.cos(ang1), -np.sin(ang1)], axis=1)
    k2 = np.arange(n2, dtype=np.float64)
    ang2 = 2.0 * np.pi * np.outer(k2, k2) / n2
    c2, s2 = np.cos(ang2), np.sin(ang2)
    f2 = np.block([[c2, -s2], [s2, c2]])
    f2i = np.block([[c2, s2], [-s2, c2]])
    angt = 2.0 * np.pi * np.outer(k1, k2) / n
    bf = tuple(jnp.asarray(a, F32).astype(BF16) for a in (f1, c3, f2, f2i))
    return bf + (jnp.asarray(np.cos(angt), F32), jnp.asarray(-np.sin(angt), F32))


def _fft_forward(x3, f1, f2, tw_re, tw_im):
    cc = x3.shape[0]
    n1 = f1.shape[0] // 2
    xb = x3.astype(BF16)
    rows = []
    for c in range(0, cc, 2):
        a = _dot(f1, jnp.concatenate([xb[c], xb[c + 1]], axis=1))
        for a_re, a_im in ((a[:n1, :FFT_N2], a[n1:, :FFT_N2]), (a[:n1, FFT_N2:], a[n1:, FFT_N2:])):
            rows.append(jnp.concatenate([a_re * tw_re - a_im * tw_im, a_re * tw_im + a_im * tw_re], axis=1))
    return _dot(jnp.concatenate(rows, axis=0).astype(BF16), f2)


def _spectrum_kernel(hf_ref, hb_ref, f1_ref, f2_ref, twre_ref, twim_ref, ore_ref, oim_ref):
    cc, nt1, n2 = hf_ref.shape
    n1 = 2 * nt1
    tabs = (f1_ref[...], f2_ref[...], twre_ref[...], twim_ref[...])
    origin = (lax.broadcasted_iota(jnp.int32, (nt1, n2), 0) == 0) & (lax.broadcasted_iota(jnp.int32, (nt1, n2), 1) == 0)
    xf = _fft_forward(hf_ref[...], *tabs)
    xb = _fft_forward(jnp.where(origin, 0.0, hb_ref[...]), *tabs)
    ore_ref[...] = (xf[:, :n2] + xb[:, :n2]).reshape(cc, n1, n2)
    oim_ref[...] = (xf[:, n2:] - xb[:, n2:]).reshape(cc, n1, n2)


def _conv_kernel(u_ref, gate_ref, bias_ref, hre_ref, him_ref, f1_ref, c3_ref, f2_ref, f2i_ref, twre_ref, twim_ref,
                 o_ref):
    _, cc, nt1, n2 = u_ref.shape
    n1 = 2 * nt1
    tw_re, tw_im = twre_ref[...], twim_ref[...]
    u = u_ref[0]
    x = _fft_forward(u, f1_ref[...], f2_ref[...], tw_re, tw_im)
    x_re, x_im = x[:, :n2], x[:, n2:]
    h_re, h_im = hre_ref[...].reshape(cc * n1, n2), him_ref[...].reshape(cc * n1, n2)
    y = jnp.concatenate([x_re * h_re - x_im * h_im, x_re * h_im + x_im * h_re], axis=1).astype(BF16)
    cm = _dot(y, f2i_ref[...])
    c3 = c3_ref[...]
    scale = 1.0 / (n1 * n2)
    for c in range(0, cc, 2):
        d = []
        for ch in (c, c + 1):
            c_re, c_im = cm[ch * n1:(ch + 1) * n1, :n2], cm[ch * n1:(ch + 1) * n1, n2:]
            d.append(jnp.concatenate([c_re * tw_re + c_im * tw_im, c_im * tw_re - c_re * tw_im], axis=0))
        yt = _dot(c3, jnp.concatenate(d, axis=1).astype(BF16))
        for k, ch in enumerate((c, c + 1)):
            conv_out = yt[:, k * n2:(k + 1) * n2] * scale + bias_ref[ch] * u[ch].astype(F32)
            o_ref[0, ch] = (gate_ref[0, ch].astype(F32) * conv_out).astype(o_ref.dtype)


def _fft_spectrum(filt4, tables):
    no, _, c, nt1, n2 = filt4.shape
    cc = FFT_CC
    ncb = c // cc
    f1, _, f2, _, tw_re, tw_im = tables
    f1 = f1[:, :nt1]

    def in_spec(direction):
        return pl.BlockSpec((None, None, cc, nt1, n2), lambda o, i: (o, direction, i, 0, 0))

    out_spec = pl.BlockSpec((cc, 2 * nt1, n2), lambda o, i: (o * ncb + i, 0, 0))
    return pl.pallas_call(
        _spectrum_kernel,
        grid=(no, ncb),
        in_specs=[in_spec(0), in_spec(1)] + [_const_spec(a.shape) for a in (f1, f2, tw_re, tw_im)],
        out_specs=[out_spec, out_spec],
        out_shape=[jax.ShapeDtypeStruct((no * c, 2 * nt1, n2), F32)] * 2,
        compiler_params=_cparams(("parallel", "parallel"), 48),
        name="hyena_spectrum",
    )(filt4, filt4, f1, f2, tw_re, tw_im)


def _fft_conv(u4, u_ch, gate4, gate_ch, bias3, hre, him, h_ch, tables):
    b, _, nt1, n2 = u4.shape
    n1 = 2 * nt1
    cc = FFT_CC
    f1, c3, f2, f2i, tw_re, tw_im = tables
    f1 = f1[:, :nt1]
    c3 = c3[:nt1]

    def seq_spec(ch0):
        return pl.BlockSpec((1, cc, nt1, n2), lambda ct, i: (i, ch0 // cc + ct, 0, 0))

    hspec = pl.BlockSpec((cc, n1, n2), lambda ct, i: (h_ch // cc + ct, 0, 0))
    return pl.pallas_call(
        _conv_kernel,
        grid=(BR_W // cc, b),
        in_specs=[seq_spec(u_ch), seq_spec(gate_ch), pl.BlockSpec((cc, 1, n2), lambda ct, i: (ct, 0, 0)), hspec, hspec]
        + [_const_spec(a.shape) for a in (f1, c3, f2, f2i, tw_re, tw_im)],
        out_specs=seq_spec(0),
        out_shape=jax.ShapeDtypeStruct((b, BR_W, nt1, n2), BF16),
        compiler_params=_cparams(("parallel", "parallel"), 48),
        name="hyena_conv",
    )(u4, gate4, bias3, hre, him, f1, c3, f2, f2i, tw_re, tw_im)


def _hyena_features(l):
    t = np.linspace(0.0, 1.0, l)[:, None]
    bands = np.linspace(1e-4, HY_BANDS - 1, HY_BANDS)[None, :]
    w = (2.0 * np.pi / l) * np.arange(l)[:, None]
    z = np.concatenate([t, np.cos(bands * w), -np.sin(bands * w)], axis=-1)
    return jnp.asarray(np.pad(z, ((0, 0), (0, LANES - HY_EMB))), F32)


def _hyena_branch(hy, lp, tables):
    b, _, l = hy.shape
    n2 = FFT_N2
    nt1 = l // n2
    pad = LANES - HY_FH
    w1 = jnp.pad(lp['hy_filt_w1'], ((0, LANES - HY_EMB), (0, pad)))
    w2 = jnp.pad(lp['hy_filt_w2'], ((0, pad), (0, pad)))
    w3t = jnp.pad(lp['hy_filt_w3'], ((0, pad), (0, 0))).T
    b1 = jnp.pad(lp['hy_filt_b1'], (0, pad)).reshape(1, LANES)
    b2 = jnp.pad(lp['hy_filt_b2'], (0, pad)).reshape(1, LANES)
    freq = jnp.pad(lp['hy_filt_freq'], ((0, 0), (0, pad)))
    dect = jnp.pad(jnp.abs(lp['hy_decay']).reshape(-1, 1), ((0, 0), (0, LANES - 1)))
    filt = _hy_filter(_hyena_features(l), w1, b1, w2, b2, freq, w3t, dect)
    hre, him = _fft_spectrum(filt.reshape(2, 2, HY_C, nt1, n2), tables)
    hy4 = hy.reshape(b, 3 * HY_C, nt1, n2)
    bias3 = jnp.broadcast_to(lp['hy_bias'][:, :, None, None], (2, HY_C, 1, n2))
    z = _fft_conv(hy4, 0, hy4, HY_C, bias3[0], hre, him, 0, tables)
    y = _fft_conv(z, 0, hy4, 2 * HY_C, bias3[1], hre, him, HY_C, tables)
    return y.reshape(b, HY_C, l)


def _mixer(x, lp, tables):
    w = lp['w_in']
    zpad = jnp.zeros((D_MODEL, LANES - 2 * GLA_R - 2 * M2_H), w.dtype)
    w_p = jnp.concatenate(
        [w[:, O_XBC:O_DT], w[:, O_M2Z:O_XBC], w[:, O_GA:O_GG], w[:, O_DT:O_GQ], zpad,
         w[:, O_GQ:O_GV], w[:, O_GV:O_GA], w[:, O_GG:O_HQ], w[:, O_HQ:O_HF], w[:, O_HF:O_HI], w[:, O_HI:O_HG],
         w[:, O_HG:O_GATE]], axis=1).astype(BF16)
    w_hyt = w[:, O_HY:O_M2Z].T.astype(BF16)
    conv_hyt = jnp.concatenate([lp['hy_conv_w'], lp['hy_conv_b'][None, :]], axis=0).T
    hy, xbc, m2z, small, gqk, gv, gg, hq, hf, hi, hgg = _inproj(
        x, lp['mix_norm_pre'], w_hyt, conv_hyt, w_p, lp['m2_conv_w'], lp['m2_conv_b'].reshape(1, N_XBC))

    y_a = _hyena_branch(hy, lp, tables)

    dt_lanes = slice(SM_DT, SM_DT + 2 * M2_H)
    dtb_row = jnp.zeros((1, LANES), F32).at[0, dt_lanes].set(lp['m2_dt_bias'])
    a_row = jnp.zeros((1, LANES), F32).at[0, dt_lanes].set(-jnp.exp(lp['m2_A_log'].astype(F32)))
    dskip_row = jnp.repeat(lp['m2_D'], M2_P).reshape(1, BR_W)
    y_b = _ssd_branch(xbc, small, m2z, dtb_row, a_row, dskip_row, lp['m2_norm'].reshape(1, BR_W))

    wg2 = jnp.zeros((2, LANES, GLA_H * GLA_DK), F32)
    for dr in range(2):
        wg2 = wg2.at[dr, SM_GA + dr * GLA_R:SM_GA + (dr + 1) * GLA_R].set(lp['gla_w_gate2'][dr])
    y_c = _gla_branch(gqk, gv, small, gg, wg2, lp['gla_b_gate'].reshape(2, 1, GLA_H * GLA_DK),
                      jnp.tile(lp['gla_norm'], GLA_H).reshape(1, BR_W))

    y_d = _hg_branch(hq, hf, hi, hgg, lp['hg_lb'].reshape(2, 1, HG_H * HG_DK),
                     jnp.tile(lp['hg_norm'], HG_H).reshape(1, BR_W))

    return _merge(x, (y_a, y_b, y_c, y_d), lp['mix_norm_pre'], lp['mix_norm_post'],
                  w[:, O_GATE:].astype(BF16), lp['w_branch'].astype(BF16), lp['w_out'].astype(BF16))


def kernel(x_prompt, x_sample, ffn1_norm_pre, ffn1_norm_post, ffn1_w_gu, ffn1_w_down, mix_norm_pre, mix_norm_post, w_in, hy_conv_w, hy_conv_b, hy_filt_w1, hy_filt_b1, hy_filt_w2, hy_filt_b2, hy_filt_freq, hy_filt_w3, hy_decay, hy_bias, m2_conv_w, m2_conv_b, m2_dt_bias, m2_A_log, m2_D, m2_norm, gla_w_gate2, gla_b_gate, gla_norm, hg_lb_param, hg_norm, w_branch, w_out, ffn2_norm_pre, ffn2_norm_post, ffn2_w_gu, ffn2_w_down):
    sm = jax.nn.softmax(hg_lb_param.astype(F32), axis=0)
    hg_lb = jnp.cumsum(sm, axis=0) - sm[0]
    params = {
        'mix_norm_pre': mix_norm_pre, 'mix_norm_post': mix_norm_post, 'w_in': w_in,
        'hy_conv_w': hy_conv_w, 'hy_conv_b': hy_conv_b,
        'hy_filt_w1': hy_filt_w1, 'hy_filt_b1': hy_filt_b1, 'hy_filt_w2': hy_filt_w2, 'hy_filt_b2': hy_filt_b2,
        'hy_filt_freq': hy_filt_freq, 'hy_filt_w3': hy_filt_w3, 'hy_decay': hy_decay, 'hy_bias': hy_bias,
        'm2_conv_w': m2_conv_w, 'm2_conv_b': m2_conv_b, 'm2_dt_bias': m2_dt_bias, 'm2_A_log': m2_A_log,
        'm2_D': m2_D, 'm2_norm': m2_norm,
        'gla_w_gate2': gla_w_gate2, 'gla_b_gate': gla_b_gate, 'gla_norm': gla_norm,
        'hg_lb': hg_lb, 'hg_norm': hg_norm, 'w_branch': w_branch, 'w_out': w_out,
    }
    assert x_prompt.shape[1:] == x_sample.shape[1:]
    x = (x_prompt, x_sample)
    depth = w_in.shape[0]
    tables = _fft_tables(2 * x_prompt.shape[1] // FFT_N2)
    for layer in range(depth):
        lp = {name: arr[layer] for name, arr in params.items()}
        x = _ffn(x, ffn1_norm_pre[layer], ffn1_norm_post[layer],
                 ffn1_w_gu[layer].astype(BF16), ffn1_w_down[layer].astype(BF16))
        x = _mixer(x, lp, tables)
        x = _ffn(x, ffn2_norm_pre[layer], ffn2_norm_post[layer],
                 ffn2_w_gu[layer].astype(BF16), ffn2_w_down[layer].astype(BF16),
                 n_first=x_prompt.shape[0], two_out=layer == depth - 1)
    return tuple(x)
```

```python
import functools
import math

import numpy as np
import jax
import jax.numpy as jnp
from jax import lax
from jax.experimental import pallas as pl
from jax.experimental.pallas import tpu as pltpu

F32 = jnp.float32
BF16 = jnp.bfloat16
EPS = 1e-6
HIGHEST = lax.Precision.HIGHEST
LOG2E = 1.4426950408889634

SUBLANES = 8
LANES = 128
VMEM_BYTES = 64 * 1024 * 1024

D_MODEL = 1024
BR_W = 256
D_FF = 2816
HY_C = 256
HY_BANDS = 16
HY_EMB = 33
HY_FH = 64
M2_H = 4
M2_P = 64
M2_N = 64
GLA_H = 4
GLA_DK = 32
GLA_DV = 64
GLA_R = 16
GLA_GATE_NORM = 16.0
HG_H = 4
HG_DK = 64
HG_DV = 64
N_GATE = 4 * D_MODEL
O_HY, O_M2Z, O_XBC, O_DT, O_GQ, O_GK, O_GV, O_GA, O_GG, O_HQ, O_HF, O_HI, O_HG, O_GATE = (
    0, 768, 1024, 1536, 1544, 1672, 1800, 2056, 2088, 2344, 2600, 3112, 3368, 3624)
N_CONV = 768 + 512
SM_GA = 0
SM_DT = 32


def _cparams(sem, vmem_mb):
    return pltpu.CompilerParams(dimension_semantics=sem, vmem_limit_bytes=vmem_mb * 1024 * 1024)


def _dot(a, b):
    return jnp.dot(a, b, preferred_element_type=F32)


def _dot_nt(a, b):
    return lax.dot_general(a, b, (((1,), (1,)), ((), ())), preferred_element_type=F32)


def _dot_tn(a, b):
    return lax.dot_general(a, b, (((0,), (0,)), ((), ())), preferred_element_type=F32)


def _rms(x, w):
    return x * lax.rsqrt(jnp.mean(x * x, axis=-1, keepdims=True) + EPS) * w


def _silu(x):
    return x * jax.nn.sigmoid(x)


def _const_spec(shape):
    nd = len(shape)
    return pl.BlockSpec(shape, lambda *_: (0,) * nd, pipeline_mode=pl.Buffered(1))


FFN_TM = 512
FFN_FC = 256


def _ffn_kernel(*refs, n_first, two_in, two_out):
    refs = list(refs)
    xa_ref = refs.pop(0)
    xb_ref = refs.pop(0) if two_in else None
    npre_ref, npost_ref, wgu_ref, wd_ref = refs[:4]
    outs = refs[4:]
    in_first = pl.program_id(0) < n_first if (two_in or two_out) else None
    x = jnp.where(in_first, xa_ref[0], xb_ref[0]) if two_in else xa_ref[0]
    xn = _rms(x, npre_ref[...]).astype(BF16)
    acc = jnp.zeros(x.shape, F32)
    for j in range(D_FF // FFN_FC):
        g = _dot(xn, wgu_ref[:, j * FFN_FC:(j + 1) * FFN_FC])
        u = _dot(xn, wgu_ref[:, D_FF + j * FFN_FC:D_FF + (j + 1) * FFN_FC])
        h = (_silu(g) * u).astype(BF16)
        acc = acc + _dot(h, wd_ref[j * FFN_FC:(j + 1) * FFN_FC, :])
    out = x + 0.5 * _rms(acc, npost_ref[...])
    if two_out:
        @pl.when(in_first)
        def _():
            outs[0][0] = out

        @pl.when(jnp.logical_not(in_first))
        def _():
            outs[1][0] = out
    else:
        outs[0][0] = out


def _ffn(xs, npre, npost, wgu, wd, n_first=None, two_out=False):
    two_in = isinstance(xs, tuple)
    if two_in:
        n_first = xs[0].shape[0]
        b = n_first + xs[1].shape[0]
    else:
        xs = (xs,)
        b = xs[0].shape[0]
    _, l, d = xs[0].shape
    tm = min(FFN_TM, l)
    nj = l // tm
    xspec = pl.BlockSpec((1, tm, d), lambda i, j: (i, j, 0))
    first_spec = pl.BlockSpec((1, tm, d), lambda i, j: (jnp.minimum(i, n_first - 1), jnp.where(i < n_first, j, nj - 1), 0))
    second_spec = pl.BlockSpec((1, tm, d), lambda i, j: (jnp.maximum(i - n_first, 0), jnp.where(i < n_first, 0, j), 0))
    if two_out:
        out_specs = [first_spec, second_spec]
        out_shape = [jax.ShapeDtypeStruct((n_first, l, d), F32), jax.ShapeDtypeStruct((b - n_first, l, d), F32)]
    else:
        out_specs, out_shape = xspec, jax.ShapeDtypeStruct((b, l, d), F32)
    return pl.pallas_call(
        functools.partial(_ffn_kernel, n_first=n_first, two_in=two_in, two_out=two_out),
        grid=(b, nj),
        in_specs=([first_spec, second_spec] if two_in else [xspec])
        + [_const_spec((1, d)), _const_spec((1, d)), _const_spec(wgu.shape), _const_spec(wd.shape)],
        out_specs=out_specs,
        out_shape=out_shape,
        compiler_params=_cparams(("arbitrary", "arbitrary") if two_out else ("parallel", "parallel"), 48),
        name="ffn",
    )(*xs, npre.reshape(1, d), npost.reshape(1, d), wgu, wd)


INP_TM = 512
_INP_OUTS = (("m2z", 256, BF16), ("small", 128, F32), ("gqk", 256, BF16), ("gv", 256, BF16), ("gg", 256, BF16),
             ("hq", 256, BF16), ("hf", 512, F32), ("hi", 256, BF16), ("hgg", 256, BF16))
N_HY = 3 * HY_C
N_XBC = BR_W + 4 * M2_N
N_INP = N_XBC + sum(w for _, w, _ in _INP_OUTS)


def _inproj_kernel(x_ref, xp_ref, xn_ref, nw_ref, wt_ref, wh_ref, cwt_ref, w_ref, cw_ref, cb_ref, hy_ref, xbc_ref,
                   *rest):
    i = pl.program_id(1)
    n = pl.num_programs(1)
    nw = nw_ref[...]
    h = _rms(x_ref[0], nw).astype(BF16)
    tm = h.shape[0]
    hpn = _rms(jnp.concatenate([xp_ref[0], xn_ref[0]], axis=0), nw).astype(BF16)
    first, last = i == 0, i == n - 1

    p = _dot_nt(wt_ref[...], h)
    halo = _dot(hpn, wh_ref[...])
    halo_t = jnp.concatenate([halo, jnp.zeros((LANES - 2 * SUBLANES, N_HY), F32)], axis=0).T
    pp = jnp.where(first, 0.0, halo_t[:, SUBLANES - 1:SUBLANES])
    pn = jnp.where(last, 0.0, halo_t[:, SUBLANES:SUBLANES + 1])
    lane = lax.broadcasted_iota(jnp.int32, (1, tm), 1)
    prev = jnp.where(lane == 0, pp, pltpu.roll(p, 1, 1))
    nxt = jnp.where(lane == tm - 1, pn, pltpu.roll(p, tm - 1, 1))
    cwt = cwt_ref[...]
    hy_ref[0] = (cwt[:, 3:4] + prev * cwt[:, 0:1] + p * cwt[:, 1:2] + nxt * cwt[:, 2:3]).astype(hy_ref.dtype)

    wc = w_ref[:, :N_XBC]
    p = _dot(h, wc)
    halo = _dot(hpn, wc)
    pp = jnp.where(first, 0.0, halo[SUBLANES - 1:SUBLANES])
    pn = jnp.where(last, 0.0, halo[SUBLANES:SUBLANES + 1])
    row = lax.broadcasted_iota(jnp.int32, (tm, 1), 0)
    prev = jnp.where(row == 0, pp, pltpu.roll(p, 1, 0))
    nxt = jnp.where(row == tm - 1, pn, pltpu.roll(p, tm - 1, 0))
    cw = cw_ref[...]
    xbc_ref[0] = _silu(cb_ref[...] + prev * cw[0:1] + p * cw[1:2] + nxt * cw[2:3]).astype(xbc_ref.dtype)
    off = N_XBC
    for (_, width, _), ref in zip(_INP_OUTS, rest):
        ref[0] = _dot(h, w_ref[:, off:off + width]).astype(ref.dtype)
        off += width


def _inproj(x, nw, w_hy, conv_hyt, w_p, conv_w, conv_b):
    b, l, d = x.shape
    tm = min(INP_TM, l)
    r8 = tm // SUBLANES
    nb8 = l // SUBLANES
    outs = (("xbc", N_XBC, BF16),) + _INP_OUTS
    w_hyt = w_hy.T
    return pl.pallas_call(
        _inproj_kernel,
        grid=(b, l // tm),
        in_specs=[pl.BlockSpec((1, tm, d), lambda i, j: (i, j, 0)),
                  pl.BlockSpec((1, SUBLANES, d), lambda i, j: (i, jnp.maximum(j * r8 - 1, 0), 0)),
                  pl.BlockSpec((1, SUBLANES, d), lambda i, j: (i, jnp.minimum((j + 1) * r8, nb8 - 1), 0)),
                  _const_spec((1, d)), _const_spec(w_hyt.shape), _const_spec(w_hy.shape), _const_spec(conv_hyt.shape),
                  _const_spec(w_p.shape), _const_spec(conv_w.shape), _const_spec(conv_b.shape)],
        out_specs=[pl.BlockSpec((1, N_HY, tm), lambda i, j: (i, 0, j))]
        + [pl.BlockSpec((1, tm, w), lambda i, j: (i, j, 0)) for _, w, _ in outs],
        out_shape=[jax.ShapeDtypeStruct((b, N_HY, l), BF16)]
        + [jax.ShapeDtypeStruct((b, l, w), dt) for _, w, dt in outs],
        compiler_params=_cparams(("parallel", "parallel"), 48),
        name="inproj",
    )(x, x, x, nw.reshape(1, d), w_hyt, w_hy, conv_hyt, w_p, conv_w, conv_b)


MRG_TM = 512


def _merge_kernel(x_ref, ya_ref, yb_ref, yc_ref, yd_ref, npre_ref, npost_ref, wg_ref, wb_ref, wo_ref, o_ref):
    x = x_ref[0]
    h = _rms(x, npre_ref[...]).astype(BF16)
    merged = jnp.zeros(x.shape, F32)
    for n, y_ref in enumerate((ya_ref, yb_ref, yc_ref, yd_ref)):
        gate = jax.nn.sigmoid(_dot(h, wg_ref[:, n * D_MODEL:(n + 1) * D_MODEL]))
        y = y_ref[0].astype(BF16)
        proj = _dot_tn(y, wb_ref[n]) if n == 0 else _dot(y, wb_ref[n])
        merged = merged + gate * proj
    out = _dot(merged.astype(BF16), wo_ref[...])
    o_ref[0] = x + _rms(out, npost_ref[...])


def _merge(x, ys, npre, npost, w_gate, w_branch, w_out):
    b, l, d = x.shape
    tm = min(MRG_TM, l)
    xspec = pl.BlockSpec((1, tm, d), lambda i, j: (i, j, 0))
    yspec = pl.BlockSpec((1, tm, BR_W), lambda i, j: (i, j, 0))
    return pl.pallas_call(
        _merge_kernel,
        grid=(b, l // tm),
        in_specs=[xspec, pl.BlockSpec((1, BR_W, tm), lambda i, j: (i, 0, j)), yspec, yspec, yspec,
                  _const_spec((1, d)), _const_spec((1, d)),
                  _const_spec(w_gate.shape), _const_spec(w_branch.shape), _const_spec(w_out.shape)],
        out_specs=xspec,
        out_shape=jax.ShapeDtypeStruct(x.shape, F32),
        compiler_params=_cparams(("parallel", "parallel"), 48),
        name="merge",
    )(x, *ys, npre.reshape(1, d), npost.reshape(1, d), w_gate, w_branch, w_out)


def _split_dot_left(m_bf16, x, passes):
    acc = None
    r = x
    for p in range(passes):
        piece = r.astype(BF16)
        t = _dot(m_bf16, piece)
        acc = t if acc is None else acc + t
        if p + 1 < passes:
            r = r - piece.astype(F32)
    return acc


def _split_dot_right(x, m_bf16, passes):
    acc = None
    r = x
    for p in range(passes):
        piece = r.astype(BF16)
        t = _dot(piece, m_bf16)
        acc = t if acc is None else acc + t
        if p + 1 < passes:
            r = r - piece.astype(F32)
    return acc


def _log_sigmoid(x):
    return jnp.minimum(x, 0.0) - jnp.log1p(jnp.exp(-jnp.abs(x)))


def _softplus(x):
    return jnp.maximum(x, 0.0) + jnp.log1p(jnp.exp(-jnp.abs(x)))


def _ones_where(mask):
    return jnp.where(mask, 1.0, 0.0).astype(BF16)


def _head_norm(o, nw_row, dv):
    cv = o.shape[1]
    r = lax.broadcasted_iota(jnp.int32, (cv, cv), 0) // dv
    s = lax.broadcasted_iota(jnp.int32, (cv, cv), 1) // dv
    ms = _split_dot_right(o * o, _ones_where(r == s), 2) * (1.0 / dv)
    return o * lax.rsqrt(ms + EPS) * nw_row


SCAN_TB = 128
SCAN_NSEQ = 3
GLA_DIAG = 8
NEG_BIG = -1e30


def _gla_levels(tb):
    out, m = [], GLA_DIAG
    while m < tb:
        out.append(m)
        m *= 2
    return out


def _gla_consts(tb, ck, dk, rev):
    c = GLA_DIAG
    nh = ck // dk
    t = np.arange(tb)
    tau = tb - 1 - t if rev else t
    tt, rr = tau[:, None], tau[None, :]
    col = np.arange(LANES)[None, :]
    ecat = col == (np.tile(np.arange(ck) // dk, c) * c + np.repeat(np.arange(c), ck))[:, None]
    spread = np.arange(LANES)[:, None] == (np.repeat(np.arange(nh), tb) * c + np.tile(tau % c, nh))[None, :]
    dmask = np.tile((tt // c == rr // c) & (rr <= tt), (1, nh))
    return jnp.asarray(rr <= tt, BF16), jnp.asarray(ecat, BF16), jnp.asarray(spread, BF16), jnp.asarray(dmask, F32)


def _interleave(gens):
    results = [None] * len(gens)
    active = list(enumerate(gens))
    while active:
        still = []
        for i, g in active:
            try:
                r = next(g)
            except StopIteration:
                continue
            if r is not None:
                results[i] = r
            still.append((i, g))
        active = still
    return results


def _gla_block(q, k, v, la, st, tril_ref, ecat_ref, spread_ref, dmask_ref, *, rev, dk, dv):
    tb, ck = q.shape
    cv = v.shape[1]
    c = GLA_DIAG
    nh = ck // dk
    levels = _gla_levels(tb)

    def rows(a, b):
        return slice(tb - b, tb - a) if rev else slice(a, b)

    def row(x):
        return rows(x, x + 1)

    cb = _split_dot_left(tril_ref[...], la, 3)
    total = cb[row(tb - 1)]
    yield
    lane_hk = lax.broadcasted_iota(jnp.int32, (1, ck), 1) // dk
    lane_hv = lax.broadcasted_iota(jnp.int32, (1, cv), 1) // dv
    vh = [jnp.where(lane_hv == h, v, 0.0) for h in range(nh)]

    qe = (q * jnp.exp(cb)).astype(BF16)
    ke = (k * jnp.exp(total - cb)).astype(BF16)
    o = _dot_nt(qe, st.astype(BF16))
    hmask = (lax.broadcasted_iota(jnp.int32, (cv, ck), 0) // dv) == (lax.broadcasted_iota(jnp.int32, (cv, ck), 1) // dk)
    st_new = st * jnp.exp(total) + jnp.where(hmask, _dot_tn(v.astype(BF16), ke), 0.0)
    yield

    g3 = (tb // c, c, ck)
    q3, k3, cb3 = q.reshape(g3), k.reshape(g3), (cb * LOG2E).reshape(g3)
    ps = []
    for j in range(c):
        jt = c - 1 - j if rev else j
        e = jnp.exp2(jnp.minimum(cb3 - jnp.broadcast_to(cb3[:, jt:jt + 1, :], g3), 0.0))
        ps.append((q3 * jnp.broadcast_to(k3[:, jt:jt + 1, :], g3) * e).reshape(tb, ck).astype(BF16))
        if j % 2 == 1:
            yield
    att8 = _dot(jnp.concatenate(ps, axis=1), ecat_ref[...]).astype(BF16)
    yield
    att = _dot(att8, spread_ref[...]) * dmask_ref[...]
    yield
    o = o + _dot(att.astype(BF16), jnp.concatenate(vh, axis=0).astype(BF16))
    yield

    tau = lax.broadcasted_iota(jnp.int32, (tb, 1), 0)
    if rev:
        tau = tb - 1 - tau
    scores = []
    for li, m in enumerate(levels):
        early = (tau % (2 * m)) < m
        npairs = tb // (2 * m)
        bnd = [None] * npairs
        for p in range(npairs):
            bnd[npairs - 1 - p if rev else p] = jnp.broadcast_to(cb[row(p * 2 * m + m - 1)], (2 * m, ck))
        diff = cb - jnp.concatenate(bnd, axis=0)
        w = jnp.exp(jnp.where(early, -diff, diff)) * jnp.where(early, k, q)
        kh = [jnp.where(lane_hk == h, w, 0.0) for h in range(nh)]
        yield
        npad = (-nh * m) % LANES
        for p in range(tb // (2 * m)):
            er, lr = rows(p * 2 * m, p * 2 * m + m), rows(p * 2 * m + m, (p + 1) * 2 * m)
            kst = [kh[h][er] for h in range(nh)] + ([jnp.zeros((npad, ck), F32)] if npad else [])
            a = _dot_nt(w[lr].astype(BF16), jnp.concatenate(kst, axis=0).astype(BF16))
            scores.append((m, p, er, a.astype(BF16)))
            yield
    for m in levels:
        npad = (-nh * m) % LANES
        parts = [None] * (tb // m)
        for mm, p, er, a in scores:
            if mm != m:
                continue
            vst = [vh[h][er] for h in range(nh)] + ([jnp.zeros((npad, cv), F32)] if npad else [])
            early_idx, late_idx = (2 * p, 2 * p + 1)
            if rev:
                early_idx, late_idx = tb // m - 1 - early_idx, tb // m - 1 - late_idx
            parts[late_idx] = _dot(a, jnp.concatenate(vst, axis=0).astype(BF16))
            parts[early_idx] = jnp.zeros((m, cv), F32)
            yield
        o = o + jnp.concatenate(parts, axis=0)
    yield o, st_new


def _reset_state(st_ref):
    @pl.when(pl.program_id(1) == 0)
    def _():
        st_ref[...] = jnp.zeros(st_ref.shape, F32)


N_SCAN_CONSTS = 4


def _gla_kernel(*refs, rev, final):
    qk_ref, v_ref, sm_ref = refs[:3]
    consts = refs[3:3 + N_SCAN_CONSTS]
    if final:
        wgh_ref, wgl_ref, bg_ref, oprev_ref, g_ref, nw_ref, o_ref, st_ref = refs[3 + N_SCAN_CONSTS:]
    else:
        wgh_ref, wgl_ref, bg_ref, o_ref, st_ref = refs[3 + N_SCAN_CONSTS:]
    _reset_state(st_ref)
    ck = GLA_H * GLA_DK
    nseq = st_ref.shape[0]
    steps = []
    for s in range(nseq):
        qk = qk_ref[s].astype(F32)
        q = qk[:, :ck] * (GLA_DK ** -0.5)
        k = qk[:, ck:]
        v = v_ref[s].astype(F32)
        a = sm_ref[s]
        a_hi = a.astype(BF16)
        a_lo = (a - a_hi.astype(F32)).astype(BF16)
        pre = _dot(a_hi, wgh_ref[...]) + _dot(a_lo, wgh_ref[...]) + _dot(a_hi, wgl_ref[...]) + bg_ref[...]
        la = _log_sigmoid(pre) * (1.0 / GLA_GATE_NORM)
        steps.append(_gla_block(q, k, v, la, st_ref[s], *consts, rev=rev, dk=GLA_DK, dv=GLA_DV))
    for s, (o, st_new) in enumerate(_interleave(steps)):
        st_ref[s] = st_new
        if final:
            o = o + oprev_ref[s]
            o = (_head_norm(o, nw_ref[...], GLA_DV) * _silu(g_ref[s].astype(F32))).astype(o_ref.dtype)
        o_ref[s] = o


def _hg_kernel(*refs, rev, final):
    q_ref, f_ref, i_ref = refs[:3]
    consts = refs[3:3 + N_SCAN_CONSTS]
    if final:
        lb_ref, oprev_ref, g_ref, nw_ref, o_ref, st_ref = refs[3 + N_SCAN_CONSTS:]
    else:
        lb_ref, o_ref, st_ref = refs[3 + N_SCAN_CONSTS:]
    _reset_state(st_ref)
    lb = lb_ref[...]
    steps = []
    for s in range(st_ref.shape[0]):
        q = q_ref[s].astype(F32) * (HG_DK ** -0.5)
        z = f_ref[s]
        pos = z >= 0.0
        e = jnp.exp(-jnp.abs(z))
        t = 1.0 + e
        r = 1.0 / t
        la = jnp.minimum(z, 0.0) - jnp.log(t) + jnp.log(1.0 + lb * jnp.where(pos, e, 1.0 / e))
        k = (1.0 - lb) * jnp.where(pos, e * r, r)
        v = i_ref[s].astype(F32)
        steps.append(_gla_block(q, k, v, la, st_ref[s], *consts, rev=rev, dk=HG_DK, dv=HG_DV))
    for s, (o, st_new) in enumerate(_interleave(steps)):
        st_ref[s] = st_new
        if final:
            o = o + oprev_ref[s]
            o = (_head_norm(o, nw_ref[...], HG_DV) * _silu(g_ref[s].astype(F32))).astype(o_ref.dtype)
        o_ref[s] = o


def _scan_call(body, name, seq_ins, const_ins, fin_seq, fin_const, scratch, b, l, rev):
    tb = min(SCAN_TB, l)
    nblk = l // tb
    final = fin_seq is not None
    nseq = SCAN_NSEQ if b % SCAN_NSEQ == 0 else 1

    def seq_spec(width, lane_idx):
        if rev:
            return pl.BlockSpec((nseq, tb, width), lambda i, j: (i, nblk - 1 - j, lane_idx))
        return pl.BlockSpec((nseq, tb, width), lambda i, j: (i, j, lane_idx))

    arrays = [a for a, _, _ in seq_ins] + list(const_ins)
    specs = [seq_spec(w, li) for _, w, li in seq_ins] + [_const_spec(a.shape) for a in const_ins]
    if final:
        arrays += [a for a, _, _ in fin_seq] + list(fin_const)
        specs += [seq_spec(w, li) for _, w, li in fin_seq] + [_const_spec(a.shape) for a in fin_const]
    return pl.pallas_call(
        functools.partial(body, rev=rev, final=final),
        grid=(b // nseq, nblk),
        in_specs=specs,
        out_specs=seq_spec(BR_W, 0),
        out_shape=jax.ShapeDtypeStruct((b, l, BR_W), BF16 if final else F32),
        scratch_shapes=[pltpu.VMEM((nseq,) + tuple(scratch), F32)],
        compiler_params=_cparams(("parallel", "arbitrary"), 32),
        name=name,
    )(*arrays)


def _gla_branch(gqk, gv, small, gg, wg2, bg, nw):
    b, l, _ = gqk.shape
    tb = min(SCAN_TB, l)
    ck = GLA_H * GLA_DK
    seq = [(gqk, 256, 0), (gv, 256, 0), (small, 128, 0)]
    scratch = (BR_W, ck)
    wg_hi = wg2.astype(BF16)
    wg_lo = (wg2 - wg_hi.astype(F32)).astype(BF16)
    o_f = _scan_call(_gla_kernel, "gla_fwd", seq, [*_gla_consts(tb, ck, GLA_DK, False), wg_hi[0], wg_lo[0], bg[0]],
                     None, None, scratch, b, l, False)
    return _scan_call(_gla_kernel, "gla_bwd", seq, [*_gla_consts(tb, ck, GLA_DK, True), wg_hi[1], wg_lo[1], bg[1]],
                      [(o_f, 256, 0), (gg, 256, 0)], [nw], scratch, b, l, True)


def _hg_branch(hq, hf, hi, hgg, lb, nw):
    b, l, _ = hq.shape
    tb = min(SCAN_TB, l)
    ck = HG_H * HG_DK
    scratch = (BR_W, ck)
    o_f = _scan_call(_hg_kernel, "hgrn_fwd", [(hq, 256, 0), (hf, 256, 0), (hi, 256, 0)],
                     [*_gla_consts(tb, ck, HG_DK, False), lb[0]], None, None, scratch, b, l, False)
    return _scan_call(_hg_kernel, "hgrn_bwd", [(hq, 256, 0), (hf, 256, 1), (hi, 256, 0)],
                      [*_gla_consts(tb, ck, HG_DK, True), lb[1]],
                      [(o_f, 256, 0), (hgg, 256, 0)], [nw], scratch, b, l, True)


def _ssd_kernel(*refs, rev, final):
    if final:
        xbc_ref, sm_ref, dtb_ref, a_ref, oprev_ref, z_ref, dskip_ref, nw_ref, o_ref, st_ref = refs
    else:
        xbc_ref, sm_ref, dtb_ref, a_ref, o_ref, st_ref = refs
    _reset_state(st_ref)
    steps = [_ssd_block(xbc_ref[s].astype(F32), sm_ref[s], dtb_ref[...], a_ref[...], st_ref[s], rev)
             for s in range(st_ref.shape[0])]
    for s, (y, x, st_new) in enumerate(_interleave(steps)):
        st_ref[s] = st_new
        if final:
            y = y + oprev_ref[s] + dskip_ref[...] * x
            o_ref[s] = _rms(y * _silu(z_ref[s].astype(F32)), nw_ref[...]).astype(o_ref.dtype)
        else:
            o_ref[s] = y


def _ssd_block(xbc, sm, dtb, a_row, st, rev):
    tb = xbc.shape[0]
    x = xbc[:, :BR_W]
    bm = xbc[:, BR_W:BR_W + 2 * M2_N].astype(BF16)
    cm = xbc[:, BR_W + 2 * M2_N:]
    dt = _softplus(sm + dtb)
    da = dt * a_row
    lane0 = SM_DT + (M2_H if rev else 0)

    t_i = lax.broadcasted_iota(jnp.int32, (tb, tb), 0)
    s_i = lax.broadcasted_iota(jnp.int32, (tb, tb), 1)
    if rev:
        incl, strict = s_i >= t_i, t_i < s_i
    else:
        incl, strict = s_i <= t_i, t_i > s_i
    m_incl = _ones_where(incl)
    edge = 0 if rev else tb - 1
    cum = _split_dot_left(m_incl, da, 3)
    suf = cum[edge:edge + 1, :] - cum
    yield
    expand = _ones_where((lax.broadcasted_iota(jnp.int32, (LANES, BR_W), 0) - lane0)
                         == (lax.broadcasted_iota(jnp.int32, (LANES, BR_W), 1) // M2_P))
    dt_x = _split_dot_right(dt, expand, 2)
    yield
    ecum_x = _split_dot_right(jnp.exp(cum), expand, 2)
    yield
    w_x = _split_dot_right(dt * jnp.exp(suf), expand, 2)
    yield
    xdt = (x * dt_x).astype(BF16)
    xw = (x * w_x).astype(BF16)

    lane_g = lax.broadcasted_iota(jnp.int32, (1, 2 * M2_N), 1) // M2_N
    lane_h = lax.broadcasted_iota(jnp.int32, (1, BR_W), 1) // M2_P
    gram = [_dot_nt(jnp.where(lane_g == g, cm, 0.0).astype(BF16), bm) for g in range(2)]
    yield
    y = jnp.zeros((tb, BR_W), F32)
    cols = [jnp.where(strict, jnp.broadcast_to(da[:, lane0 + h:lane0 + h + 1], (tb, tb)), 0.0) for h in range(M2_H)]
    e_all = _split_dot_left(m_incl, jnp.concatenate(cols, axis=1), 3)
    yield
    for h in range(M2_H):
        seg = jnp.exp(jnp.where(incl, e_all[:, h * tb:(h + 1) * tb], NEG_BIG))
        scores = (gram[h // (M2_H // 2)] * seg).astype(BF16)
        y = y + jnp.where(lane_h == h, _dot(scores, xdt), 0.0)
        yield
    y = y + ecum_x * _dot(cm.astype(BF16), st.astype(BF16))
    yield
    gmask = ((lax.broadcasted_iota(jnp.int32, st.shape, 0) // M2_N)
             == (lax.broadcasted_iota(jnp.int32, st.shape, 1) // (2 * M2_P)))
    yield y, x, st * ecum_x[edge:edge + 1, :] + jnp.where(gmask, _dot_tn(bm, xw), 0.0)


def _ssd_branch(xbc, small, m2z, dtb_row, a_row, dskip_row, nw):
    b, l, _ = xbc.shape
    seq = [(xbc, 512, 0), (small, 128, 0)]
    shape = (2 * M2_N, BR_W)
    o_f = _scan_call(_ssd_kernel, "ssd_fwd", seq, [dtb_row, a_row], None, None, shape, b, l, False)
    return _scan_call(_ssd_kernel, "ssd_bwd", seq, [dtb_row, a_row], [(o_f, 256, 0), (m2z, 256, 0)],
                      [dskip_row, nw], shape, b, l, True)


FFT_N2 = 128
FFT_CC = 16
HYF_TL = 512


def _hyfilt_kernel(z_ref, w1_ref, b1_ref, w2_ref, b2_ref, fr_ref, w3t_ref, dect_ref, o_ref):
    def hdot(a, b):
        return jnp.dot(a, b, preferred_element_type=F32, precision=HIGHEST)

    def hdot_nt(a, b):
        return lax.dot_general(a, b, (((1,), (1,)), ((), ())), preferred_element_type=F32, precision=HIGHEST)

    z = z_ref[...]
    h = jnp.sin(fr_ref[0:1, :] * (hdot(z, w1_ref[...]) + b1_ref[...]))
    h = jnp.sin(fr_ref[1:2, :] * (hdot(h, w2_ref[...]) + b2_ref[...]))
    o_ref[...] = hdot_nt(w3t_ref[...], h) * jnp.exp(-hdot_nt(dect_ref[...], z))


def _hy_filter(zfeat, w1, b1, w2, b2, freq, w3t, dect):
    l = zfeat.shape[0]
    tl = min(HYF_TL, l)
    nout = w3t.shape[0]
    return pl.pallas_call(
        _hyfilt_kernel,
        grid=(l // tl,),
        in_specs=[pl.BlockSpec((tl, LANES), lambda i: (i, 0))]
        + [_const_spec(a.shape) for a in (w1, b1, w2, b2, freq, w3t, dect)],
        out_specs=pl.BlockSpec((nout, tl), lambda i: (0, i)),
        out_shape=jax.ShapeDtypeStruct((nout, l), F32),
        compiler_params=_cparams(("parallel",), 32),
        name="hyena_filter",
    )(zfeat, w1, b1, w2, b2, freq, w3t, dect)


def _fft_tables(n1):
    n2 = FFT_N2
    n = n1 * n2
    k1 = np.arange(n1, dtype=np.float64)
    ang1 = 2.0 * np.pi * np.outer(k1, k1) / n1
    f1 = np.concatenate([np.cos(ang1), -np.sin(ang1)], axis=0)
    c3 = np.concatenate([np.cos(ang1), -np.sin(ang1)], axis=1)
    k2 = np.arange(n2, dtype=np.float64)
    ang2 = 2.0 * np.pi * np.outer(k2, k2) / n2
    c2, s2 = np.cos(ang2), np.sin(ang2)
    f2 = np.block([[c2, -s2], [s2, c2]])
    f2i = np.block([[c2, s2], [-s2, c2]])
    angt = 2.0 * np.pi * np.outer(k1, k2) / n
    bf = tuple(jnp.asarray(a, F32).astype(BF16) for a in (f1, c3, f2, f2i))
    return bf + (jnp.asarray(np.cos(angt), F32), jnp.asarray(-np.sin(angt), F32))


def _fft_forward(x3, f1, f2, tw_re, tw_im):
    cc = x3.shape[0]
    n1 = f1.shape[0] // 2
    xb = x3.astype(BF16)
    rows = []
    for c in range(0, cc, 2):
        a = _dot(f1, jnp.concatenate([xb[c], xb[c + 1]], axis=1))
        for a_re, a_im in ((a[:n1, :FFT_N2], a[n1:, :FFT_N2]), (a[:n1, FFT_N2:], a[n1:, FFT_N2:])):
            rows.append(jnp.concatenate([a_re * tw_re - a_im * tw_im, a_re * tw_im + a_im * tw_re], axis=1))
    return _dot(jnp.concatenate(rows, axis=0).astype(BF16), f2)


def _spectrum_kernel(hf_ref, hb_ref, f1_ref, f2_ref, twre_ref, twim_ref, ore_ref, oim_ref):
    cc, nt1, n2 = hf_ref.shape
    n1 = 2 * nt1
    tabs = (f1_ref[...], f2_ref[...], twre_ref[...], twim_ref[...])
    origin = (lax.broadcasted_iota(jnp.int32, (nt1, n2), 0) == 0) & (lax.broadcasted_iota(jnp.int32, (nt1, n2), 1) == 0)
    xf = _fft_forward(hf_ref[...], *tabs)
    xb = _fft_forward(jnp.where(origin, 0.0, hb_ref[...]), *tabs)
    ore_ref[...] = (xf[:, :n2] + xb[:, :n2]).reshape(cc, n1, n2)
    oim_ref[...] = (xf[:, n2:] - xb[:, n2:]).reshape(cc, n1, n2)


def _conv_kernel(u_ref, gate_ref, bias_ref, hre_ref, him_ref, f1_ref, c3_ref, f2_ref, f2i_ref, twre_ref, twim_ref,
                 o_ref):
    _, cc, nt1, n2 = u_ref.shape
    n1 = 2 * nt1
    tw_re, tw_im = twre_ref[...], twim_ref[...]
    u = u_ref[0]
    x = _fft_forward(u, f1_ref[...], f2_ref[...], tw_re, tw_im)
    x_re, x_im = x[:, :n2], x[:, n2:]
    h_re, h_im = hre_ref[...].reshape(cc * n1, n2), him_ref[...].reshape(cc * n1, n2)
    y = jnp.concatenate([x_re * h_re - x_im * h_im, x_re * h_im + x_im * h_re], axis=1).astype(BF16)
    cm = _dot(y, f2i_ref[...])
    c3 = c3_ref[...]
    scale = 1.0 / (n1 * n2)
    for c in range(0, cc, 2):
        d = []
        for ch in (c, c + 1):
            c_re, c_im = cm[ch * n1:(ch + 1) * n1, :n2], cm[ch * n1:(ch + 1) * n1, n2:]
            d.append(jnp.concatenate([c_re * tw_re + c_im * tw_im, c_im * tw_re - c_re * tw_im], axis=0))
        yt = _dot(c3, jnp.concatenate(d, axis=1).astype(BF16))
        for k, ch in enumerate((c, c + 1)):
            conv_out = yt[:, k * n2:(k + 1) * n2] * scale + bias_ref[ch] * u[ch].astype(F32)
            o_ref[0, ch] = (gate_ref[0, ch].astype(F32) * conv_out).astype(o_ref.dtype)


def _fft_spectrum(filt4, tables):
    no, _, c, nt1, n2 = filt4.shape
    cc = FFT_CC
    ncb = c // cc
    f1, _, f2, _, tw_re, tw_im = tables
    f1 = f1[:, :nt1]

    def in_spec(direction):
        return pl.BlockSpec((None, None, cc, nt1, n2), lambda o, i: (o, direction, i, 0, 0))

    out_spec = pl.BlockSpec((cc, 2 * nt1, n2), lambda o, i: (o * ncb + i, 0, 0))
    return pl.pallas_call(
        _spectrum_kernel,
        grid=(no, ncb),
        in_specs=[in_spec(0), in_spec(1)] + [_const_spec(a.shape) for a in (f1, f2, tw_re, tw_im)],
        out_specs=[out_spec, out_spec],
        out_shape=[jax.ShapeDtypeStruct((no * c, 2 * nt1, n2), F32)] * 2,
        compiler_params=_cparams(("parallel", "parallel"), 48),
        name="hyena_spectrum",
    )(filt4, filt4, f1, f2, tw_re, tw_im)


def _fft_conv(u4, u_ch, gate4, gate_ch, bias3, hre, him, h_ch, tables):
    b, _, nt1, n2 = u4.shape
    n1 = 2 * nt1
    cc = FFT_CC
    f1, c3, f2, f2i, tw_re, tw_im = tables
    f1 = f1[:, :nt1]
    c3 = c3[:nt1]

    def seq_spec(ch0):
        return pl.BlockSpec((1, cc, nt1, n2), lambda ct, i: (i, ch0 // cc + ct, 0, 0))

    hspec = pl.BlockSpec((cc, n1, n2), lambda ct, i: (h_ch // cc + ct, 0, 0))
    return pl.pallas_call(
        _conv_kernel,
        grid=(BR_W // cc, b),
        in_specs=[seq_spec(u_ch), seq_spec(gate_ch), pl.BlockSpec((cc, 1, n2), lambda ct, i: (ct, 0, 0)), hspec, hspec]
        + [_const_spec(a.shape) for a in (f1, c3, f2, f2i, tw_re, tw_im)],
        out_specs=seq_spec(0),
        out_shape=jax.ShapeDtypeStruct((b, BR_W, nt1, n2), BF16),
        compiler_params=_cparams(("parallel", "parallel"), 48),
        name="hyena_conv",
    )(u4, gate4, bias3, hre, him, f1, c3, f2, f2i, tw_re, tw_im)


def _hyena_features(l):
    t = np.linspace(0.0, 1.0, l)[:, None]
    bands = np.linspace(1e-4, HY_BANDS - 1, HY_BANDS)[None, :]
    w = (2.0 * np.pi / l) * np.arange(l)[:, None]
    z = np.concatenate([t, np.cos(bands * w), -np.sin(bands * w)], axis=-1)
    return jnp.asarray(np.pad(z, ((0, 0), (0, LANES - HY_EMB))), F32)


def _hyena_branch(hy, lp, tables):
    b, _, l = hy.shape
    n2 = FFT_N2
    nt1 = l // n2
    pad = LANES - HY_FH
    w1 = jnp.pad(lp['hy_filt_w1'], ((0, LANES - HY_EMB), (0, pad)))
    w2 = jnp.pad(lp['hy_filt_w2'], ((0, pad), (0, pad)))
    w3t = jnp.pad(lp['hy_filt_w3'], ((0, pad), (0, 0))).T
    b1 = jnp.pad(lp['hy_filt_b1'], (0, pad)).reshape(1, LANES)
    b2 = jnp.pad(lp['hy_filt_b2'], (0, pad)).reshape(1, LANES)
    freq = jnp.pad(lp['hy_filt_freq'], ((0, 0), (0, pad)))
    dect = jnp.pad(jnp.abs(lp['hy_decay']).reshape(-1, 1), ((0, 0), (0, LANES - 1)))
    filt = _hy_filter(_hyena_features(l), w1, b1, w2, b2, freq, w3t, dect)
    hre, him = _fft_spectrum(filt.reshape(2, 2, HY_C, nt1, n2), tables)
    hy4 = hy.reshape(b, 3 * HY_C, nt1, n2)
    bias3 = jnp.broadcast_to(lp['hy_bias'][:, :, None, None], (2, HY_C, 1, n2))
    z = _fft_conv(hy4, 0, hy4, HY_C, bias3[0], hre, him, 0, tables)
    y = _fft_conv(z, 0, hy4, 2 * HY_C, bias3[1], hre, him, HY_C, tables)
    return y.reshape(b, HY_C, l)


def _mixer(x, lp, tables):
    w = lp['w_in']
    zpad = jnp.zeros((D_MODEL, LANES - 2 * GLA_R - 2 * M2_H), w.dtype)
    w_p = jnp.concatenate(
        [w[:, O_XBC:O_DT], w[:, O_M2Z:O_XBC], w[:, O_GA:O_GG], w[:, O_DT:O_GQ], zpad,
         w[:, O_GQ:O_GV], w[:, O_GV:O_GA], w[:, O_GG:O_HQ], w[:, O_HQ:O_HF], w[:, O_HF:O_HI], w[:, O_HI:O_HG],
         w[:, O_HG:O_GATE]], axis=1).astype(BF16)
    conv_hyt = jnp.concatenate([lp['hy_conv_w'], lp['hy_conv_b'][None, :]], axis=0).T
    hy, xbc, m2z, small, gqk, gv, gg, hq, hf, hi, hgg = _inproj(
        x, lp['mix_norm_pre'], w[:, O_HY:O_M2Z].astype(BF16), conv_hyt, w_p, lp['m2_conv_w'],
        lp['m2_conv_b'].reshape(1, N_XBC))

    y_a = _hyena_branch(hy, lp, tables)

    dt_lanes = slice(SM_DT, SM_DT + 2 * M2_H)
    dtb_row = jnp.zeros((1, LANES), F32).at[0, dt_lanes].set(lp['m2_dt_bias'])
    a_row = jnp.zeros((1, LANES), F32).at[0, dt_lanes].set(-jnp.exp(lp['m2_A_log'].astype(F32)))
    dskip_row = jnp.repeat(lp['m2_D'], M2_P).reshape(1, BR_W)
    y_b = _ssd_branch(xbc, small, m2z, dtb_row, a_row, dskip_row, lp['m2_norm'].reshape(1, BR_W))

    wg2 = jnp.zeros((2, LANES, GLA_H * GLA_DK), F32)
    for dr in range(2):
        wg2 = wg2.at[dr, SM_GA + dr * GLA_R:SM_GA + (dr + 1) * GLA_R].set(lp['gla_w_gate2'][dr])
    y_c = _gla_branch(gqk, gv, small, gg, wg2, lp['gla_b_gate'].reshape(2, 1, GLA_H * GLA_DK),
                      jnp.tile(lp['gla_norm'], GLA_H).reshape(1, BR_W))

    y_d = _hg_branch(hq, hf, hi, hgg, lp['hg_lb'].reshape(2, 1, HG_H * HG_DK),
                     jnp.tile(lp['hg_norm'], HG_H).reshape(1, BR_W))

    return _merge(x, (y_a, y_b, y_c, y_d), lp['mix_norm_pre'], lp['mix_norm_post'],
                  w[:, O_GATE:].astype(BF16), lp['w_branch'].astype(BF16), lp['w_out'].astype(BF16))


def kernel(x_prompt, x_sample, ffn1_norm_pre, ffn1_norm_post, ffn1_w_gu, ffn1_w_down, mix_norm_pre, mix_norm_post, w_in, hy_conv_w, hy_conv_b, hy_filt_w1, hy_filt_b1, hy_filt_w2, hy_filt_b2, hy_filt_freq, hy_filt_w3, hy_decay, hy_bias, m2_conv_w, m2_conv_b, m2_dt_bias, m2_A_log, m2_D, m2_norm, gla_w_gate2, gla_b_gate, gla_norm, hg_lb_param, hg_norm, w_branch, w_out, ffn2_norm_pre, ffn2_norm_post, ffn2_w_gu, ffn2_w_down):
    sm = jax.nn.softmax(hg_lb_param.astype(F32), axis=0)
    hg_lb = jnp.cumsum(sm, axis=0) - sm[0]
    params = {
        'mix_norm_pre': mix_norm_pre, 'mix_norm_post': mix_norm_post, 'w_in': w_in,
        'hy_conv_w': hy_conv_w, 'hy_conv_b': hy_conv_b,
        'hy_filt_w1': hy_filt_w1, 'hy_filt_b1': hy_filt_b1, 'hy_filt_w2': hy_filt_w2, 'hy_filt_b2': hy_filt_b2,
        'hy_filt_freq': hy_filt_freq, 'hy_filt_w3': hy_filt_w3, 'hy_decay': hy_decay, 'hy_bias': hy_bias,
        'm2_conv_w': m2_conv_w, 'm2_conv_b': m2_conv_b, 'm2_dt_bias': m2_dt_bias, 'm2_A_log': m2_A_log,
        'm2_D': m2_D, 'm2_norm': m2_norm,
        'gla_w_gate2': gla_w_gate2, 'gla_b_gate': gla_b_gate, 'gla_norm': gla_norm,
        'hg_lb': hg_lb, 'hg_norm': hg_norm, 'w_branch': w_branch, 'w_out': w_out,
    }
    assert x_prompt.shape[1:] == x_sample.shape[1:]
    x = (x_prompt, x_sample)
    depth = w_in.shape[0]
    tables = _fft_tables(2 * x_prompt.shape[1] // FFT_N2)
    for layer in range(depth):
        lp = {name: arr[layer] for name, arr in params.items()}
        x = _ffn(x, ffn1_norm_pre[layer], ffn1_norm_post[layer],
                 ffn1_w_gu[layer].astype(BF16), ffn1_w_down[layer].astype(BF16))
        x = _mixer(x, lp, tables)
        x = _ffn(x, ffn2_norm_pre[layer], ffn2_norm_post[layer],
                 ffn2_w_gu[layer].astype(BF16), ffn2_w_down[layer].astype(BF16),
                 n_first=x_prompt.shape[0], two_out=layer == depth - 1)
    return tuple(x)
```

```python
import functools
import math

import numpy as np
import jax
import jax.numpy as jnp
from jax import lax
from jax.experimental import pallas as pl
from jax.experimental.pallas import tpu as pltpu

F32 = jnp.float32
BF16 = jnp.bfloat16
EPS = 1e-6
LOG2E = 1.4426950408889634

SUBLANES = 8
LANES = 128
VMEM_BYTES = 64 * 1024 * 1024

D_MODEL = 1024
BR_W = 256
D_FF = 2816
HY_C = 256
HY_BANDS = 16
HY_EMB = 33
HY_FH = 64
M2_H = 4
M2_P = 64
M2_N = 64
GLA_H = 4
GLA_DK = 32
GLA_DV = 64
GLA_R = 16
GLA_GATE_NORM = 16.0
HG_H = 4
HG_DK = 64
HG_DV = 64
N_GATE = 4 * D_MODEL
O_HY, O_M2Z, O_XBC, O_DT, O_GQ, O_GK, O_GV, O_GA, O_GG, O_HQ, O_HF, O_HI, O_HG, O_GATE = (
    0, 768, 1024, 1536, 1544, 1672, 1800, 2056, 2088, 2344, 2600, 3112, 3368, 3624)
N_CONV = 768 + 512
SM_GA = 0
SM_DT = 32


def _cparams(sem, vmem_mb):
    return pltpu.CompilerParams(dimension_semantics=sem, vmem_limit_bytes=vmem_mb * 1024 * 1024)


def _dot(a, b):
    return jnp.dot(a, b, preferred_element_type=F32)


def _dot_nt(a, b):
    return lax.dot_general(a, b, (((1,), (1,)), ((), ())), preferred_element_type=F32)


def _dot_tn(a, b):
    return lax.dot_general(a, b, (((0,), (0,)), ((), ())), preferred_element_type=F32)


def _rms(x, w):
    return x * lax.rsqrt(jnp.mean(x * x, axis=-1, keepdims=True) + EPS) * w


def _silu(x):
    return x * jax.nn.sigmoid(x)


def _const_spec(shape):
    nd = len(shape)
    return pl.BlockSpec(shape, lambda *_: (0,) * nd, pipeline_mode=pl.Buffered(1))


FFN_TM = 1024
FFN_SUB = 512
FFN_FC = 256


def _ffn_kernel(*refs, n_first, two_in, two_out):
    refs = list(refs)
    xa_ref = refs.pop(0)
    xb_ref = refs.pop(0) if two_in else None
    npre_ref, npost_ref, wgu_ref, wd_ref = refs[:4]
    outs = refs[4:]
    in_first = pl.program_id(0) < n_first if (two_in or two_out) else None
    x_tile = jnp.where(in_first, xa_ref[0], xb_ref[0]) if two_in else xa_ref[0]
    pieces = []
    sub = min(FFN_SUB, x_tile.shape[0])
    for r in range(x_tile.shape[0] // sub):
        x = x_tile[r * sub:(r + 1) * sub]
        xn = _rms(x, npre_ref[...]).astype(BF16)
        acc = jnp.zeros(x.shape, F32)
        for j in range(D_FF // FFN_FC):
            g = _dot(xn, wgu_ref[:, j * FFN_FC:(j + 1) * FFN_FC])
            u = _dot(xn, wgu_ref[:, D_FF + j * FFN_FC:D_FF + (j + 1) * FFN_FC])
            h = (_silu(g) * u).astype(BF16)
            acc = acc + _dot(h, wd_ref[j * FFN_FC:(j + 1) * FFN_FC, :])
        pieces.append(x + 0.5 * _rms(acc, npost_ref[...]))
    out = jnp.concatenate(pieces, axis=0)
    if two_out:
        @pl.when(in_first)
        def _():
            outs[0][0] = out

        @pl.when(jnp.logical_not(in_first))
        def _():
            outs[1][0] = out
    else:
        outs[0][0] = out


def _ffn(xs, npre, npost, wgu, wd, n_first=None, two_out=False):
    two_in = isinstance(xs, tuple)
    if two_in:
        n_first = xs[0].shape[0]
        b = n_first + xs[1].shape[0]
    else:
        xs = (xs,)
        b = xs[0].shape[0]
    _, l, d = xs[0].shape
    tm = min(FFN_TM, l)
    nj = l // tm
    xspec = pl.BlockSpec((1, tm, d), lambda i, j: (i, j, 0))
    first_spec = pl.BlockSpec((1, tm, d), lambda i, j: (jnp.minimum(i, n_first - 1), jnp.where(i < n_first, j, nj - 1), 0))
    second_spec = pl.BlockSpec((1, tm, d), lambda i, j: (jnp.maximum(i - n_first, 0), jnp.where(i < n_first, 0, j), 0))
    if two_out:
        out_specs = [first_spec, second_spec]
        out_shape = [jax.ShapeDtypeStruct((n_first, l, d), F32), jax.ShapeDtypeStruct((b - n_first, l, d), F32)]
    else:
        out_specs, out_shape = xspec, jax.ShapeDtypeStruct((b, l, d), F32)
    return pl.pallas_call(
        functools.partial(_ffn_kernel, n_first=n_first, two_in=two_in, two_out=two_out),
        grid=(b, nj),
        in_specs=([first_spec, second_spec] if two_in else [xspec])
        + [_const_spec((1, d)), _const_spec((1, d)), _const_spec(wgu.shape), _const_spec(wd.shape)],
        out_specs=out_specs,
        out_shape=out_shape,
        compiler_params=_cparams(("arbitrary", "arbitrary") if two_out else ("parallel", "parallel"), 56),
        name="ffn",
    )(*xs, npre.reshape(1, d), npost.reshape(1, d), wgu, wd)


INP_TM = 512
_INP_OUTS = (("m2z", 256, BF16), ("small", 128, F32), ("gqk", 256, BF16), ("gv", 256, BF16), ("gg", 256, BF16),
             ("hq", 256, BF16), ("hf", 512, F32), ("hi", 256, BF16), ("hgg", 256, BF16))
N_HY = 3 * HY_C
N_XBC = BR_W + 4 * M2_N
N_INP = N_XBC + sum(w for _, w, _ in _INP_OUTS)


def _inproj_kernel(x_ref, xp_ref, xn_ref, nw_ref, wt_ref, wh_ref, cwt_ref, w_ref, cw_ref, cb_ref, hy_ref, xbc_ref,
                   *rest):
    i = pl.program_id(1)
    n = pl.num_programs(1)
    nw = nw_ref[...]
    h = _rms(x_ref[0], nw).astype(BF16)
    tm = h.shape[0]
    hpn = _rms(jnp.concatenate([xp_ref[0], xn_ref[0]], axis=0), nw).astype(BF16)
    first, last = i == 0, i == n - 1

    p = _dot_nt(wt_ref[...], h)
    halo = _dot(hpn, wh_ref[...])
    halo_t = jnp.concatenate([halo, jnp.zeros((LANES - 2 * SUBLANES, N_HY), F32)], axis=0).T
    pp = jnp.where(first, 0.0, halo_t[:, SUBLANES - 1:SUBLANES])
    pn = jnp.where(last, 0.0, halo_t[:, SUBLANES:SUBLANES + 1])
    lane = lax.broadcasted_iota(jnp.int32, (1, tm), 1)
    prev = jnp.where(lane == 0, pp, pltpu.roll(p, 1, 1))
    nxt = jnp.where(lane == tm - 1, pn, pltpu.roll(p, tm - 1, 1))
    cwt = cwt_ref[...]
    hy_ref[0] = (cwt[:, 3:4] + prev * cwt[:, 0:1] + p * cwt[:, 1:2] + nxt * cwt[:, 2:3]).astype(hy_ref.dtype)

    wc = w_ref[:, :N_XBC]
    p = _dot(h, wc)
    halo = _dot(hpn, wc)
    pp = jnp.where(first, 0.0, halo[SUBLANES - 1:SUBLANES])
    pn = jnp.where(last, 0.0, halo[SUBLANES:SUBLANES + 1])
    row = lax.broadcasted_iota(jnp.int32, (tm, 1), 0)
    prev = jnp.where(row == 0, pp, pltpu.roll(p, 1, 0))
    nxt = jnp.where(row == tm - 1, pn, pltpu.roll(p, tm - 1, 0))
    cw = cw_ref[...]
    xbc_ref[0] = _silu(cb_ref[...] + prev * cw[0:1] + p * cw[1:2] + nxt * cw[2:3]).astype(xbc_ref.dtype)
    off = N_XBC
    for (_, width, _), ref in zip(_INP_OUTS, rest):
        ref[0] = _dot(h, w_ref[:, off:off + width]).astype(ref.dtype)
        off += width


def _inproj(x, nw, w_hy, conv_hyt, w_p, conv_w, conv_b):
    b, l, d = x.shape
    tm = min(INP_TM, l)
    r8 = tm // SUBLANES
    nb8 = l // SUBLANES
    outs = (("xbc", N_XBC, BF16),) + _INP_OUTS
    w_hyt = w_hy.T
    return pl.pallas_call(
        _inproj_kernel,
        grid=(b, l // tm),
        in_specs=[pl.BlockSpec((1, tm, d), lambda i, j: (i, j, 0)),
                  pl.BlockSpec((1, SUBLANES, d), lambda i, j: (i, jnp.maximum(j * r8 - 1, 0), 0)),
                  pl.BlockSpec((1, SUBLANES, d), lambda i, j: (i, jnp.minimum((j + 1) * r8, nb8 - 1), 0)),
                  _const_spec((1, d)), _const_spec(w_hyt.shape), _const_spec(w_hy.shape), _const_spec(conv_hyt.shape),
                  _const_spec(w_p.shape), _const_spec(conv_w.shape), _const_spec(conv_b.shape)],
        out_specs=[pl.BlockSpec((1, N_HY, tm), lambda i, j: (i, 0, j))]
        + [pl.BlockSpec((1, tm, w), lambda i, j: (i, j, 0)) for _, w, _ in outs],
        out_shape=[jax.ShapeDtypeStruct((b, N_HY, l), BF16)]
        + [jax.ShapeDtypeStruct((b, l, w), dt) for _, w, dt in outs],
        compiler_params=_cparams(("parallel", "parallel"), 48),
        name="inproj",
    )(x, x, x, nw.reshape(1, d), w_hyt, w_hy, conv_hyt, w_p, conv_w, conv_b)


MRG_TM = 512


def _merge_kernel(x_ref, ya_ref, yb_ref, yc_ref, yd_ref, npre_ref, npost_ref, wg_ref, wb_ref, wo_ref, o_ref):
    x = x_ref[0]
    h = _rms(x, npre_ref[...]).astype(BF16)
    merged = jnp.zeros(x.shape, F32)
    for n, y_ref in enumerate((ya_ref, yb_ref, yc_ref, yd_ref)):
        gate = jax.nn.sigmoid(_dot(h, wg_ref[:, n * D_MODEL:(n + 1) * D_MODEL]))
        y = y_ref[0].astype(BF16)
        proj = _dot_tn(y, wb_ref[n]) if n == 0 else _dot(y, wb_ref[n])
        merged = merged + gate * proj
    out = _dot(merged.astype(BF16), wo_ref[...])
    o_ref[0] = x + _rms(out, npost_ref[...])


def _merge(x, ys, npre, npost, w_gate, w_branch, w_out):
    b, l, d = x.shape
    tm = min(MRG_TM, l)
    xspec = pl.BlockSpec((1, tm, d), lambda i, j: (i, j, 0))
    yspec = pl.BlockSpec((1, tm, BR_W), lambda i, j: (i, j, 0))
    return pl.pallas_call(
        _merge_kernel,
        grid=(b, l // tm),
        in_specs=[xspec, pl.BlockSpec((1, BR_W, tm), lambda i, j: (i, 0, j)), yspec, yspec, yspec,
                  _const_spec((1, d)), _const_spec((1, d)),
                  _const_spec(w_gate.shape), _const_spec(w_branch.shape), _const_spec(w_out.shape)],
        out_specs=xspec,
        out_shape=jax.ShapeDtypeStruct(x.shape, F32),
        compiler_params=_cparams(("parallel", "parallel"), 48),
        name="merge",
    )(x, *ys, npre.reshape(1, d), npost.reshape(1, d), w_gate, w_branch, w_out)


def _split_dot_left(m_bf16, x, passes):
    acc = None
    r = x
    for p in range(passes):
        piece = r.astype(BF16)
        t = _dot(m_bf16, piece)
        acc = t if acc is None else acc + t
        if p + 1 < passes:
            r = r - piece.astype(F32)
    return acc


def _split_dot_right(x, m_bf16, passes):
    acc = None
    r = x
    for p in range(passes):
        piece = r.astype(BF16)
        t = _dot(piece, m_bf16)
        acc = t if acc is None else acc + t
        if p + 1 < passes:
            r = r - piece.astype(F32)
    return acc


def _log_sigmoid(x):
    return jnp.minimum(x, 0.0) - jnp.log1p(jnp.exp(-jnp.abs(x)))


def _softplus(x):
    return jnp.maximum(x, 0.0) + jnp.log1p(jnp.exp(-jnp.abs(x)))


def _ones_where(mask):
    return jnp.where(mask, 1.0, 0.0).astype(BF16)


def _head_norm(o, nw_row, dv):
    cv = o.shape[1]
    r = lax.broadcasted_iota(jnp.int32, (cv, cv), 0) // dv
    s = lax.broadcasted_iota(jnp.int32, (cv, cv), 1) // dv
    ms = _split_dot_right(o * o, _ones_where(r == s), 2) * (1.0 / dv)
    return o * lax.rsqrt(ms + EPS) * nw_row


SCAN_TB = 128
SCAN_NSEQ = {"gla": 6, "hgrn": 3, "ssd": 6}
GLA_DIAG = 8
NEG_BIG = -1e30


def _gla_levels(tb):
    out, m = [], GLA_DIAG
    while m < tb:
        out.append(m)
        m *= 2
    return out


def _gla_consts(tb, ck, dk, rev):
    c = GLA_DIAG
    nh = ck // dk
    t = np.arange(tb)
    tau = tb - 1 - t if rev else t
    tt, rr = tau[:, None], tau[None, :]
    col = np.arange(LANES)[None, :]
    ecat = col == (np.tile(np.arange(ck) // dk, c) * c + np.repeat(np.arange(c), ck))[:, None]
    spread = np.arange(LANES)[:, None] == (np.repeat(np.arange(nh), tb) * c + np.tile(tau % c, nh))[None, :]
    dmask = np.tile((tt // c == rr // c) & (rr <= tt), (1, nh))
    return jnp.asarray(rr <= tt, BF16), jnp.asarray(ecat, BF16), jnp.asarray(spread, BF16), jnp.asarray(dmask, F32)


def _interleave(gens):
    results = [None] * len(gens)
    active = list(enumerate(gens))
    while active:
        still = []
        for i, g in active:
            try:
                r = next(g)
            except StopIteration:
                continue
            if r is not None:
                results[i] = r
            still.append((i, g))
        active = still
    return results


def _gla_block(q, k, v, la, st, tril_ref, ecat_ref, spread_ref, dmask_ref, *, rev, dk, dv):
    tb, ck = q.shape
    cv = v.shape[1]
    c = GLA_DIAG
    nh = ck // dk
    levels = _gla_levels(tb)

    def rows(a, b):
        return slice(tb - b, tb - a) if rev else slice(a, b)

    def row(x):
        return rows(x, x + 1)

    cb = _split_dot_left(tril_ref[...], la, 3)
    total = cb[row(tb - 1)]
    yield
    lane_hk = lax.broadcasted_iota(jnp.int32, (1, ck), 1) // dk
    lane_hv = lax.broadcasted_iota(jnp.int32, (1, cv), 1) // dv
    vh = [jnp.where(lane_hv == h, v, 0.0) for h in range(nh)]

    qe = (q * jnp.exp(cb)).astype(BF16)
    ke = (k * jnp.exp(total - cb)).astype(BF16)
    o = _dot_nt(qe, st.astype(BF16))
    hmask = (lax.broadcasted_iota(jnp.int32, (cv, ck), 0) // dv) == (lax.broadcasted_iota(jnp.int32, (cv, ck), 1) // dk)
    st_new = st * jnp.exp(total) + jnp.where(hmask, _dot_tn(v.astype(BF16), ke), 0.0)
    yield

    g3 = (tb // c, c, ck)
    q3, k3, cb3 = q.reshape(g3), k.reshape(g3), (cb * LOG2E).reshape(g3)
    ps = []
    for j in range(c):
        jt = c - 1 - j if rev else j
        e = jnp.exp2(jnp.minimum(cb3 - jnp.broadcast_to(cb3[:, jt:jt + 1, :], g3), 0.0))
        ps.append((q3 * jnp.broadcast_to(k3[:, jt:jt + 1, :], g3) * e).reshape(tb, ck).astype(BF16))
        if j % 2 == 1:
            yield
    att8 = _dot(jnp.concatenate(ps, axis=1), ecat_ref[...]).astype(BF16)
    yield
    att = _dot(att8, spread_ref[...]) * dmask_ref[...]
    yield
    o = o + _dot(att.astype(BF16), jnp.concatenate(vh, axis=0).astype(BF16))
    yield

    tau = lax.broadcasted_iota(jnp.int32, (tb, 1), 0)
    if rev:
        tau = tb - 1 - tau
    scores = []
    for li, m in enumerate(levels):
        early = (tau % (2 * m)) < m
        npairs = tb // (2 * m)
        bnd = [None] * npairs
        for p in range(npairs):
            bnd[npairs - 1 - p if rev else p] = jnp.broadcast_to(cb[row(p * 2 * m + m - 1)], (2 * m, ck))
        diff = cb - jnp.concatenate(bnd, axis=0)
        w = jnp.exp(jnp.where(early, -diff, diff)) * jnp.where(early, k, q)
        yield
        npad = (-nh * m) % LANES
        for p in range(tb // (2 * m)):
            er, lr = rows(p * 2 * m, p * 2 * m + m), rows(p * 2 * m + m, (p + 1) * 2 * m)
            kst = [jnp.where(lane_hk == h, w[er], 0.0) for h in range(nh)]
            kst += [jnp.zeros((npad, ck), F32)] if npad else []
            a = _dot_nt(w[lr].astype(BF16), jnp.concatenate(kst, axis=0).astype(BF16))
            scores.append((m, p, er, a.astype(BF16)))
            yield
    for m in levels:
        npad = (-nh * m) % LANES
        parts = [None] * (tb // m)
        for mm, p, er, a in scores:
            if mm != m:
                continue
            vst = [vh[h][er] for h in range(nh)] + ([jnp.zeros((npad, cv), F32)] if npad else [])
            early_idx, late_idx = (2 * p, 2 * p + 1)
            if rev:
                early_idx, late_idx = tb // m - 1 - early_idx, tb // m - 1 - late_idx
            parts[late_idx] = _dot(a, jnp.concatenate(vst, axis=0).astype(BF16))
            parts[early_idx] = jnp.zeros((m, cv), F32)
            yield
        o = o + jnp.concatenate(parts, axis=0)
    yield o, st_new


def _reset_state(st_ref):
    @pl.when(pl.program_id(1) == 0)
    def _():
        st_ref[...] = jnp.zeros(st_ref.shape, F32)


N_SCAN_CONSTS = 4


def _gla_kernel(*refs, rev, final):
    qk_ref, v_ref, sm_ref = refs[:3]
    consts = refs[3:3 + N_SCAN_CONSTS]
    if final:
        wgh_ref, wgl_ref, bg_ref, oprev_ref, g_ref, nw_ref, o_ref, st_ref = refs[3 + N_SCAN_CONSTS:]
    else:
        wgh_ref, wgl_ref, bg_ref, o_ref, st_ref = refs[3 + N_SCAN_CONSTS:]
    _reset_state(st_ref)
    ck = GLA_H * GLA_DK
    nseq = st_ref.shape[0]
    steps = []
    for s in range(nseq):
        qk = qk_ref[s].astype(F32)
        q = qk[:, :ck] * (GLA_DK ** -0.5)
        k = qk[:, ck:]
        v = v_ref[s].astype(F32)
        a = sm_ref[s]
        a_hi = a.astype(BF16)
        a_lo = (a - a_hi.astype(F32)).astype(BF16)
        pre = _dot(a_hi, wgh_ref[...]) + _dot(a_lo, wgh_ref[...]) + _dot(a_hi, wgl_ref[...]) + bg_ref[...]
        la = _log_sigmoid(pre) * (1.0 / GLA_GATE_NORM)
        steps.append(_gla_block(q, k, v, la, st_ref[s], *consts, rev=rev, dk=GLA_DK, dv=GLA_DV))
    for s, (o, st_new) in enumerate(_interleave(steps)):
        st_ref[s] = st_new
        if final:
            o = o + oprev_ref[s]
            o = (_head_norm(o, nw_ref[...], GLA_DV) * _silu(g_ref[s].astype(F32))).astype(o_ref.dtype)
        o_ref[s] = o


def _hg_kernel(*refs, rev, final):
    q_ref, f_ref, i_ref = refs[:3]
    consts = refs[3:3 + N_SCAN_CONSTS]
    if final:
        lb_ref, oprev_ref, g_ref, nw_ref, o_ref, st_ref = refs[3 + N_SCAN_CONSTS:]
    else:
        lb_ref, o_ref, st_ref = refs[3 + N_SCAN_CONSTS:]
    _reset_state(st_ref)
    lb = lb_ref[...]
    steps = []
    for s in range(st_ref.shape[0]):
        q = q_ref[s].astype(F32) * (HG_DK ** -0.5)
        z = f_ref[s]
        pos = z >= 0.0
        e = jnp.exp(-jnp.abs(z))
        t = 1.0 + e
        r = 1.0 / t
        la = jnp.minimum(z, 0.0) - jnp.log(t) + jnp.log(1.0 + lb * jnp.where(pos, e, 1.0 / e))
        k = (1.0 - lb) * jnp.where(pos, e * r, r)
        v = i_ref[s].astype(F32)
        steps.append(_gla_block(q, k, v, la, st_ref[s], *consts, rev=rev, dk=HG_DK, dv=HG_DV))
    for s, (o, st_new) in enumerate(_interleave(steps)):
        st_ref[s] = st_new
        if final:
            o = o + oprev_ref[s]
            o = (_head_norm(o, nw_ref[...], HG_DV) * _silu(g_ref[s].astype(F32))).astype(o_ref.dtype)
        o_ref[s] = o


def _scan_call(body, name, seq_ins, const_ins, fin_seq, fin_const, scratch, b, l, rev):
    tb = min(SCAN_TB, l)
    nblk = l // tb
    final = fin_seq is not None
    nseq = math.gcd(b, SCAN_NSEQ[name.split("_")[0]])

    def seq_spec(width, lane_idx):
        if rev:
            return pl.BlockSpec((nseq, tb, width), lambda i, j: (i, nblk - 1 - j, lane_idx))
        return pl.BlockSpec((nseq, tb, width), lambda i, j: (i, j, lane_idx))

    arrays = [a for a, _, _ in seq_ins] + list(const_ins)
    specs = [seq_spec(w, li) for _, w, li in seq_ins] + [_const_spec(a.shape) for a in const_ins]
    if final:
        arrays += [a for a, _, _ in fin_seq] + list(fin_const)
        specs += [seq_spec(w, li) for _, w, li in fin_seq] + [_const_spec(a.shape) for a in fin_const]
    return pl.pallas_call(
        functools.partial(body, rev=rev, final=final),
        grid=(b // nseq, nblk),
        in_specs=specs,
        out_specs=seq_spec(BR_W, 0),
        out_shape=jax.ShapeDtypeStruct((b, l, BR_W), BF16 if final else F32),
        scratch_shapes=[pltpu.VMEM((nseq,) + tuple(scratch), F32)],
        compiler_params=_cparams(("parallel", "arbitrary"), 32),
        name=name,
    )(*arrays)


def _gla_branch(gqk, gv, small, gg, wg2, bg, nw):
    b, l, _ = gqk.shape
    tb = min(SCAN_TB, l)
    ck = GLA_H * GLA_DK
    seq = [(gqk, 256, 0), (gv, 256, 0), (small, 128, 0)]
    scratch = (BR_W, ck)
    wg_hi = wg2.astype(BF16)
    wg_lo = (wg2 - wg_hi.astype(F32)).astype(BF16)
    o_f = _scan_call(_gla_kernel, "gla_fwd", seq, [*_gla_consts(tb, ck, GLA_DK, False), wg_hi[0], wg_lo[0], bg[0]],
                     None, None, scratch, b, l, False)
    return _scan_call(_gla_kernel, "gla_bwd", seq, [*_gla_consts(tb, ck, GLA_DK, True), wg_hi[1], wg_lo[1], bg[1]],
                      [(o_f, 256, 0), (gg, 256, 0)], [nw], scratch, b, l, True)


def _hg_branch(hq, hf, hi, hgg, lb, nw):
    b, l, _ = hq.shape
    tb = min(SCAN_TB, l)
    ck = HG_H * HG_DK
    scratch = (BR_W, ck)
    o_f = _scan_call(_hg_kernel, "hgrn_fwd", [(hq, 256, 0), (hf, 256, 0), (hi, 256, 0)],
                     [*_gla_consts(tb, ck, HG_DK, False), lb[0]], None, None, scratch, b, l, False)
    return _scan_call(_hg_kernel, "hgrn_bwd", [(hq, 256, 0), (hf, 256, 1), (hi, 256, 0)],
                      [*_gla_consts(tb, ck, HG_DK, True), lb[1]],
                      [(o_f, 256, 0), (hgg, 256, 0)], [nw], scratch, b, l, True)


def _ssd_kernel(*refs, rev, final):
    if final:
        xbc_ref, sm_ref, dtb_ref, a_ref, oprev_ref, z_ref, dskip_ref, nw_ref, o_ref, st_ref = refs
    else:
        xbc_ref, sm_ref, dtb_ref, a_ref, o_ref, st_ref = refs
    _reset_state(st_ref)
    steps = [_ssd_block(xbc_ref[s].astype(F32), sm_ref[s], dtb_ref[...], a_ref[...], st_ref[s], rev)
             for s in range(st_ref.shape[0])]
    for s, (y, x, st_new) in enumerate(_interleave(steps)):
        st_ref[s] = st_new
        if final:
            y = y + oprev_ref[s] + dskip_ref[...] * x
            o_ref[s] = _rms(y * _silu(z_ref[s].astype(F32)), nw_ref[...]).astype(o_ref.dtype)
        else:
            o_ref[s] = y


def _ssd_block(xbc, sm, dtb, a_row, st, rev):
    tb = xbc.shape[0]
    x = xbc[:, :BR_W]
    bm = xbc[:, BR_W:BR_W + 2 * M2_N].astype(BF16)
    cm = xbc[:, BR_W + 2 * M2_N:]
    dt = _softplus(sm + dtb)
    da = dt * a_row
    lane0 = SM_DT + (M2_H if rev else 0)

    t_i = lax.broadcasted_iota(jnp.int32, (tb, tb), 0)
    s_i = lax.broadcasted_iota(jnp.int32, (tb, tb), 1)
    if rev:
        incl, strict = s_i >= t_i, t_i < s_i
    else:
        incl, strict = s_i <= t_i, t_i > s_i
    m_incl = _ones_where(incl)
    edge = 0 if rev else tb - 1
    cum = _split_dot_left(m_incl, da, 3)
    suf = cum[edge:edge + 1, :] - cum
    yield
    expand = _ones_where((lax.broadcasted_iota(jnp.int32, (LANES, BR_W), 0) - lane0)
                         == (lax.broadcasted_iota(jnp.int32, (LANES, BR_W), 1) // M2_P))
    dt_x = _split_dot_right(dt, expand, 2)
    yield
    ecum_x = _split_dot_right(jnp.exp(cum), expand, 2)
    yield
    w_x = _split_dot_right(dt * jnp.exp(suf), expand, 2)
    yield
    xdt = (x * dt_x).astype(BF16)
    xw = (x * w_x).astype(BF16)

    lane_g = lax.broadcasted_iota(jnp.int32, (1, 2 * M2_N), 1) // M2_N
    lane_h = lax.broadcasted_iota(jnp.int32, (1, BR_W), 1) // M2_P
    gram = [_dot_nt(jnp.where(lane_g == g, cm, 0.0).astype(BF16), bm) for g in range(2)]
    yield
    y = jnp.zeros((tb, BR_W), F32)
    cols = [jnp.where(strict, jnp.broadcast_to(da[:, lane0 + h:lane0 + h + 1], (tb, tb)), 0.0) for h in range(M2_H)]
    e_all = _split_dot_left(m_incl, jnp.concatenate(cols, axis=1), 3)
    yield
    for h in range(M2_H):
        seg = jnp.exp(jnp.where(incl, e_all[:, h * tb:(h + 1) * tb], NEG_BIG))
        scores = (gram[h // (M2_H // 2)] * seg).astype(BF16)
        y = y + jnp.where(lane_h == h, _dot(scores, xdt), 0.0)
        yield
    y = y + ecum_x * _dot(cm.astype(BF16), st.astype(BF16))
    yield
    gmask = ((lax.broadcasted_iota(jnp.int32, st.shape, 0) // M2_N)
             == (lax.broadcasted_iota(jnp.int32, st.shape, 1) // (2 * M2_P)))
    yield y, x, st * ecum_x[edge:edge + 1, :] + jnp.where(gmask, _dot_tn(bm, xw), 0.0)


def _ssd_branch(xbc, small, m2z, dtb_row, a_row, dskip_row, nw):
    b, l, _ = xbc.shape
    seq = [(xbc, 512, 0), (small, 128, 0)]
    shape = (2 * M2_N, BR_W)
    o_f = _scan_call(_ssd_kernel, "ssd_fwd", seq, [dtb_row, a_row], None, None, shape, b, l, False)
    return _scan_call(_ssd_kernel, "ssd_bwd", seq, [dtb_row, a_row], [(o_f, 256, 0), (m2z, 256, 0)],
                      [dskip_row, nw], shape, b, l, True)


FFT_N2 = 128
FFT_CC = 16
HYF_TL = 512


def _hi_lo(x):
    hi = x.astype(BF16)
    return hi, (x - hi.astype(F32)).astype(BF16)


def _hyfilt_kernel(z_ref, t_ref, w1h_ref, w1l_ref, b1_ref, w2h_ref, w2l_ref, b2_ref, fr_ref, w3h_ref, w3l_ref,
                   dec_ref, o_ref):
    def dot3(a, bh_ref, bl_ref, dot):
        ah, al = _hi_lo(a)
        return dot(ah, bh_ref[...]) + dot(al, bh_ref[...]) + dot(ah, bl_ref[...])

    h = jnp.sin(fr_ref[0:1, :] * (dot3(z_ref[...], w1h_ref, w1l_ref, _dot) + b1_ref[...]))
    h = jnp.sin(fr_ref[1:2, :] * (dot3(h, w2h_ref, w2l_ref, _dot) + b2_ref[...]))
    window = jnp.exp(-dec_ref[...] * t_ref[...])
    o_ref[...] = dot3(h, w3h_ref, w3l_ref, lambda a, b: _dot_nt(b, a)) * window


def _hy_filter(zfeat, trow, w1, b1, w2, b2, freq, w3t, dec_col):
    l = zfeat.shape[0]
    tl = min(HYF_TL, l)
    nout = w3t.shape[0]
    consts = (*_hi_lo(w1), b1, *_hi_lo(w2), b2, freq, *_hi_lo(w3t), dec_col)
    return pl.pallas_call(
        _hyfilt_kernel,
        grid=(l // tl,),
        in_specs=[pl.BlockSpec((tl, LANES), lambda i: (i, 0)), pl.BlockSpec((1, tl), lambda i: (0, i))]
        + [_const_spec(a.shape) for a in consts],
        out_specs=pl.BlockSpec((nout, tl), lambda i: (0, i)),
        out_shape=jax.ShapeDtypeStruct((nout, l), F32),
        compiler_params=_cparams(("parallel",), 32),
        name="hyena_filter",
    )(zfeat, trow, *consts)


def _fft_tables(n1):
    n2 = FFT_N2
    n = n1 * n2
    k1 = np.arange(n1, dtype=np.float64)
    ang1 = 2.0 * np.pi * np.outer(k1, k1) / n1
    f1 = np.concatenate([np.cos(ang1), -np.sin(ang1)], axis=0)
    c3 = np.concatenate([np.cos(ang1), -np.sin(ang1)], axis=1)
    k2 = np.arange(n2, dtype=np.float64)
    ang2 = 2.0 * np.pi * np.outer(k2, k2) / n2
    c2, s2 = np.cos(ang2), np.sin(ang2)
    f2 = np.block([[c2, -s2], [s2, c2]])
    f2i = np.block([[c2, s2], [-s2, c2]])
    angt = 2.0 * np.pi * np.outer(k1, k2) / n
    bf = tuple(jnp.asarray(a, F32).astype(BF16) for a in (f1, c3, f2, f2i))
    return bf + (jnp.asarray(np.cos(angt), F32), jnp.asarray(-np.sin(angt), F32))


def _fft_forward(x3, f1, f2, tw_re, tw_im):
    cc = x3.shape[0]
    n1 = f1.shape[0] // 2
    xb = x3.astype(BF16)
    rows = []
    for c in range(0, cc, 2):
        a = _dot(f1, jnp.concatenate([xb[c], xb[c + 1]], axis=1))
        for a_re, a_im in ((a[:n1, :FFT_N2], a[n1:, :FFT_N2]), (a[:n1, FFT_N2:], a[n1:, FFT_N2:])):
            rows.append(jnp.concatenate([a_re * tw_re - a_im * tw_im, a_re * tw_im + a_im * tw_re], axis=1))
    return _dot(jnp.concatenate(rows, axis=0).astype(BF16), f2)


def _spectrum_kernel(hf_ref, hb_ref, f1_ref, f2_ref, twre_ref, twim_ref, ore_ref, oim_ref):
    cc, nt1, n2 = hf_ref.shape
    n1 = 2 * nt1
    tabs = (f1_ref[...], f2_ref[...], twre_ref[...], twim_ref[...])
    origin = (lax.broadcasted_iota(jnp.int32, (nt1, n2), 0) == 0) & (lax.broadcasted_iota(jnp.int32, (nt1, n2), 1) == 0)
    xf = _fft_forward(hf_ref[...], *tabs)
    xb = _fft_forward(jnp.where(origin, 0.0, hb_ref[...]), *tabs)
    ore_ref[...] = (xf[:, :n2] + xb[:, :n2]).reshape(cc, n1, n2)
    oim_ref[...] = (xf[:, n2:] - xb[:, n2:]).reshape(cc, n1, n2)


def _conv_kernel(u_ref, gate_ref, bias_ref, hre_ref, him_ref, f1_ref, c3_ref, f2_ref, f2i_ref, twre_ref, twim_ref,
                 o_ref):
    _, cc, nt1, n2 = u_ref.shape
    n1 = 2 * nt1
    tw_re, tw_im = twre_ref[...], twim_ref[...]
    u = u_ref[0]
    x = _fft_forward(u, f1_ref[...], f2_ref[...], tw_re, tw_im)
    x_re, x_im = x[:, :n2], x[:, n2:]
    h_re, h_im = hre_ref[...].reshape(cc * n1, n2), him_ref[...].reshape(cc * n1, n2)
    y = jnp.concatenate([x_re * h_re - x_im * h_im, x_re * h_im + x_im * h_re], axis=1).astype(BF16)
    cm = _dot(y, f2i_ref[...])
    c3 = c3_ref[...]
    scale = 1.0 / (n1 * n2)
    for c in range(0, cc, 2):
        d = []
        for ch in (c, c + 1):
            c_re, c_im = cm[ch * n1:(ch + 1) * n1, :n2], cm[ch * n1:(ch + 1) * n1, n2:]
            d.append(jnp.concatenate([c_re * tw_re + c_im * tw_im, c_im * tw_re - c_re * tw_im], axis=0))
        yt = _dot(c3, jnp.concatenate(d, axis=1).astype(BF16))
        for k, ch in enumerate((c, c + 1)):
            conv_out = yt[:, k * n2:(k + 1) * n2] * scale + bias_ref[ch] * u[ch].astype(F32)
            o_ref[0, ch] = (gate_ref[0, ch].astype(F32) * conv_out).astype(o_ref.dtype)


def _fft_spectrum(filt4, tables):
    no, _, c, nt1, n2 = filt4.shape
    cc = FFT_CC
    ncb = c // cc
    f1, _, f2, _, tw_re, tw_im = tables
    f1 = f1[:, :nt1]

    def in_spec(direction):
        return pl.BlockSpec((None, None, cc, nt1, n2), lambda o, i: (o, direction, i, 0, 0))

    out_spec = pl.BlockSpec((cc, 2 * nt1, n2), lambda o, i: (o * ncb + i, 0, 0))
    return pl.pallas_call(
        _spectrum_kernel,
        grid=(no, ncb),
        in_specs=[in_spec(0), in_spec(1)] + [_const_spec(a.shape) for a in (f1, f2, tw_re, tw_im)],
        out_specs=[out_spec, out_spec],
        out_shape=[jax.ShapeDtypeStruct((no * c, 2 * nt1, n2), F32)] * 2,
        compiler_params=_cparams(("parallel", "parallel"), 48),
        name="hyena_spectrum",
    )(filt4, filt4, f1, f2, tw_re, tw_im)


def _fft_conv(u4, u_ch, gate4, gate_ch, bias3, hre, him, h_ch, tables):
    b, _, nt1, n2 = u4.shape
    n1 = 2 * nt1
    cc = FFT_CC
    f1, c3, f2, f2i, tw_re, tw_im = tables
    f1 = f1[:, :nt1]
    c3 = c3[:nt1]

    def seq_spec(ch0):
        return pl.BlockSpec((1, cc, nt1, n2), lambda ct, i: (i, ch0 // cc + ct, 0, 0))

    hspec = pl.BlockSpec((cc, n1, n2), lambda ct, i: (h_ch // cc + ct, 0, 0))
    return pl.pallas_call(
        _conv_kernel,
        grid=(BR_W // cc, b),
        in_specs=[seq_spec(u_ch), seq_spec(gate_ch), pl.BlockSpec((cc, 1, n2), lambda ct, i: (ct, 0, 0)), hspec, hspec]
        + [_const_spec(a.shape) for a in (f1, c3, f2, f2i, tw_re, tw_im)],
        out_specs=seq_spec(0),
        out_shape=jax.ShapeDtypeStruct((b, BR_W, nt1, n2), BF16),
        compiler_params=_cparams(("parallel", "parallel"), 48),
        name="hyena_conv",
    )(u4, gate4, bias3, hre, him, f1, c3, f2, f2i, tw_re, tw_im)


def _hyena_features(l):
    t = np.linspace(0.0, 1.0, l)[:, None]
    bands = np.linspace(1e-4, HY_BANDS - 1, HY_BANDS)[None, :]
    w = (2.0 * np.pi / l) * np.arange(l)[:, None]
    z = np.concatenate([t, np.cos(bands * w), -np.sin(bands * w)], axis=-1)
    return jnp.asarray(np.pad(z, ((0, 0), (0, LANES - HY_EMB))), F32), jnp.asarray(t.T, F32)


def _hyena_branch(hy, lp, tables):
    b, _, l = hy.shape
    n2 = FFT_N2
    nt1 = l // n2
    pad = LANES - HY_FH
    w1 = jnp.pad(lp['hy_filt_w1'], ((0, LANES - HY_EMB), (0, pad)))
    w2 = jnp.pad(lp['hy_filt_w2'], ((0, pad), (0, pad)))
    w3t = jnp.pad(lp['hy_filt_w3'], ((0, pad), (0, 0))).T
    b1 = jnp.pad(lp['hy_filt_b1'], (0, pad)).reshape(1, LANES)
    b2 = jnp.pad(lp['hy_filt_b2'], (0, pad)).reshape(1, LANES)
    freq = jnp.pad(lp['hy_filt_freq'], ((0, 0), (0, pad)))
    dec_col = jnp.abs(lp['hy_decay']).reshape(-1, 1)
    filt = _hy_filter(*_hyena_features(l), w1, b1, w2, b2, freq, w3t, dec_col)
    hre, him = _fft_spectrum(filt.reshape(2, 2, HY_C, nt1, n2), tables)
    hy4 = hy.reshape(b, 3 * HY_C, nt1, n2)
    bias3 = jnp.broadcast_to(lp['hy_bias'][:, :, None, None], (2, HY_C, 1, n2))
    z = _fft_conv(hy4, 0, hy4, HY_C, bias3[0], hre, him, 0, tables)
    y = _fft_conv(z, 0, hy4, 2 * HY_C, bias3[1], hre, him, HY_C, tables)
    return y.reshape(b, HY_C, l)


def _mixer(x, lp, tables):
    w = lp['w_in']
    zpad = jnp.zeros((D_MODEL, LANES - 2 * GLA_R - 2 * M2_H), w.dtype)
    w_p = jnp.concatenate(
        [w[:, O_XBC:O_DT], w[:, O_M2Z:O_XBC], w[:, O_GA:O_GG], w[:, O_DT:O_GQ], zpad,
         w[:, O_GQ:O_GV], w[:, O_GV:O_GA], w[:, O_GG:O_HQ], w[:, O_HQ:O_HF], w[:, O_HF:O_HI], w[:, O_HI:O_HG],
         w[:, O_HG:O_GATE]], axis=1).astype(BF16)
    conv_hyt = jnp.concatenate([lp['hy_conv_w'], lp['hy_conv_b'][None, :]], axis=0).T
    hy, xbc, m2z, small, gqk, gv, gg, hq, hf, hi, hgg = _inproj(
        x, lp['mix_norm_pre'], w[:, O_HY:O_M2Z].astype(BF16), conv_hyt, w_p, lp['m2_conv_w'],
        lp['m2_conv_b'].reshape(1, N_XBC))

    y_a = _hyena_branch(hy, lp, tables)

    dt_lanes = slice(SM_DT, SM_DT + 2 * M2_H)
    dtb_row = jnp.zeros((1, LANES), F32).at[0, dt_lanes].set(lp['m2_dt_bias'])
    a_row = jnp.zeros((1, LANES), F32).at[0, dt_lanes].set(-jnp.exp(lp['m2_A_log'].astype(F32)))
    dskip_row = jnp.repeat(lp['m2_D'], M2_P).reshape(1, BR_W)
    y_b = _ssd_branch(xbc, small, m2z, dtb_row, a_row, dskip_row, lp['m2_norm'].reshape(1, BR_W))

    wg2 = jnp.zeros((2, LANES, GLA_H * GLA_DK), F32)
    for dr in range(2):
        wg2 = wg2.at[dr, SM_GA + dr * GLA_R:SM_GA + (dr + 1) * GLA_R].set(lp['gla_w_gate2'][dr])
    y_c = _gla_branch(gqk, gv, small, gg, wg2, lp['gla_b_gate'].reshape(2, 1, GLA_H * GLA_DK),
                      jnp.tile(lp['gla_norm'], GLA_H).reshape(1, BR_W))

    y_d = _hg_branch(hq, hf, hi, hgg, lp['hg_lb'].reshape(2, 1, HG_H * HG_DK),
                     jnp.tile(lp['hg_norm'], HG_H).reshape(1, BR_W))

    return _merge(x, (y_a, y_b, y_c, y_d), lp['mix_norm_pre'], lp['mix_norm_post'],
                  w[:, O_GATE:].astype(BF16), lp['w_branch'].astype(BF16), lp['w_out'].astype(BF16))


def kernel(x_prompt, x_sample, ffn1_norm_pre, ffn1_norm_post, ffn1_w_gu, ffn1_w_down, mix_norm_pre, mix_norm_post, w_in, hy_conv_w, hy_conv_b, hy_filt_w1, hy_filt_b1, hy_filt_w2, hy_filt_b2, hy_filt_freq, hy_filt_w3, hy_decay, hy_bias, m2_conv_w, m2_conv_b, m2_dt_bias, m2_A_log, m2_D, m2_norm, gla_w_gate2, gla_b_gate, gla_norm, hg_lb_param, hg_norm, w_branch, w_out, ffn2_norm_pre, ffn2_norm_post, ffn2_w_gu, ffn2_w_down):
    sm = jax.nn.softmax(hg_lb_param.astype(F32), axis=0)
    hg_lb = jnp.cumsum(sm, axis=0) - sm[0]
    params = {
        'mix_norm_pre': mix_norm_pre, 'mix_norm_post': mix_norm_post, 'w_in': w_in,
        'hy_conv_w': hy_conv_w, 'hy_conv_b': hy_conv_b,
        'hy_filt_w1': hy_filt_w1, 'hy_filt_b1': hy_filt_b1, 'hy_filt_w2': hy_filt_w2, 'hy_filt_b2': hy_filt_b2,
        'hy_filt_freq': hy_filt_freq, 'hy_filt_w3': hy_filt_w3, 'hy_decay': hy_decay, 'hy_bias': hy_bias,
        'm2_conv_w': m2_conv_w, 'm2_conv_b': m2_conv_b, 'm2_dt_bias': m2_dt_bias, 'm2_A_log': m2_A_log,
        'm2_D': m2_D, 'm2_norm': m2_norm,
        'gla_w_gate2': gla_w_gate2, 'gla_b_gate': gla_b_gate, 'gla_norm': gla_norm,
        'hg_lb': hg_lb, 'hg_norm': hg_norm, 'w_branch': w_branch, 'w_out': w_out,
    }
    assert x_prompt.shape[1:] == x_sample.shape[1:]
    x = (x_prompt, x_sample)
    depth = w_in.shape[0]
    tables = _fft_tables(2 * x_prompt.shape[1] // FFT_N2)
    for layer in range(depth):
        lp = {name: arr[layer] for name, arr in params.items()}
        x = _ffn(x, ffn1_norm_pre[layer], ffn1_norm_post[layer],
                 ffn1_w_gu[layer].astype(BF16), ffn1_w_down[layer].astype(BF16))
        x = _mixer(x, lp, tables)
        x = _ffn(x, ffn2_norm_pre[layer], ffn2_norm_post[layer],
                 ffn2_w_gu[layer].astype(BF16), ffn2_w_down[layer].astype(BF16),
                 n_first=x_prompt.shape[0], two_out=layer == depth - 1)
    return tuple(x)
```

```python
import functools
import math

import numpy as np
import jax
import jax.numpy as jnp
from jax import lax
from jax.experimental import pallas as pl
from jax.experimental.pallas import tpu as pltpu

F32 = jnp.float32
BF16 = jnp.bfloat16
EPS = 1e-6
LOG2E = 1.4426950408889634

SUBLANES = 8
LANES = 128
VMEM_BYTES = 64 * 1024 * 1024

D_MODEL = 1024
BR_W = 256
D_FF = 2816
HY_C = 256
HY_BANDS = 16
HY_EMB = 33
HY_FH = 64
M2_H = 4
M2_P = 64
M2_N = 64
GLA_H = 4
GLA_DK = 32
GLA_DV = 64
GLA_R = 16
GLA_GATE_NORM = 16.0
HG_H = 4
HG_DK = 64
HG_DV = 64
N_GATE = 4 * D_MODEL
O_HY, O_M2Z, O_XBC, O_DT, O_GQ, O_GK, O_GV, O_GA, O_GG, O_HQ, O_HF, O_HI, O_HG, O_GATE = (
    0, 768, 1024, 1536, 1544, 1672, 1800, 2056, 2088, 2344, 2600, 3112, 3368, 3624)
N_CONV = 768 + 512
SM_GA = 0
SM_DT = 32


def _cparams(sem, vmem_mb):
    return pltpu.CompilerParams(dimension_semantics=sem, vmem_limit_bytes=vmem_mb * 1024 * 1024)


def _dot(a, b):
    return jnp.dot(a, b, preferred_element_type=F32)


def _dot_nt(a, b):
    return lax.dot_general(a, b, (((1,), (1,)), ((), ())), preferred_element_type=F32)


def _dot_tn(a, b):
    return lax.dot_general(a, b, (((0,), (0,)), ((), ())), preferred_element_type=F32)


def _rms(x, w):
    return x * lax.rsqrt(jnp.mean(x * x, axis=-1, keepdims=True) + EPS) * w


def _silu(x):
    return x * jax.nn.sigmoid(x)


def _const_spec(shape):
    nd = len(shape)
    return pl.BlockSpec(shape, lambda *_: (0,) * nd, pipeline_mode=pl.Buffered(1))


FFN_TM = 1024
FFN_SUB = 512
FFN_FC = 256


def _ffn_kernel(*refs, n_first, two_in, two_out):
    refs = list(refs)
    xa_ref = refs.pop(0)
    xb_ref = refs.pop(0) if two_in else None
    npre_ref, npost_ref, wgu_ref, wd_ref = refs[:4]
    outs = refs[4:]
    in_first = pl.program_id(0) < n_first if (two_in or two_out) else None
    x_tile = jnp.where(in_first, xa_ref[0], xb_ref[0]) if two_in else xa_ref[0]
    pieces = []
    sub = min(FFN_SUB, x_tile.shape[0])
    for r in range(x_tile.shape[0] // sub):
        x = x_tile[r * sub:(r + 1) * sub]
        xn = _rms(x, npre_ref[...]).astype(BF16)
        acc = jnp.zeros(x.shape, F32)
        for j in range(D_FF // FFN_FC):
            g = _dot(xn, wgu_ref[:, j * FFN_FC:(j + 1) * FFN_FC])
            u = _dot(xn, wgu_ref[:, D_FF + j * FFN_FC:D_FF + (j + 1) * FFN_FC])
            h = (_silu(g) * u).astype(BF16)
            acc = acc + _dot(h, wd_ref[j * FFN_FC:(j + 1) * FFN_FC, :])
        pieces.append(x + 0.5 * _rms(acc, npost_ref[...]))
    out = jnp.concatenate(pieces, axis=0)
    if two_out:
        @pl.when(in_first)
        def _():
            outs[0][0] = out

        @pl.when(jnp.logical_not(in_first))
        def _():
            outs[1][0] = out
    else:
        outs[0][0] = out


def _ffn(xs, npre, npost, wgu, wd, n_first=None, two_out=False):
    two_in = isinstance(xs, tuple)
    if two_in:
        n_first = xs[0].shape[0]
        b = n_first + xs[1].shape[0]
    else:
        xs = (xs,)
        b = xs[0].shape[0]
    _, l, d = xs[0].shape
    tm = min(FFN_TM, l)
    nj = l // tm
    xspec = pl.BlockSpec((1, tm, d), lambda i, j: (i, j, 0))
    first_spec = pl.BlockSpec((1, tm, d), lambda i, j: (jnp.minimum(i, n_first - 1), jnp.where(i < n_first, j, nj - 1), 0))
    second_spec = pl.BlockSpec((1, tm, d), lambda i, j: (jnp.maximum(i - n_first, 0), jnp.where(i < n_first, 0, j), 0))
    if two_out:
        out_specs = [first_spec, second_spec]
        out_shape = [jax.ShapeDtypeStruct((n_first, l, d), F32), jax.ShapeDtypeStruct((b - n_first, l, d), F32)]
    else:
        out_specs, out_shape = xspec, jax.ShapeDtypeStruct((b, l, d), F32)
    return pl.pallas_call(
        functools.partial(_ffn_kernel, n_first=n_first, two_in=two_in, two_out=two_out),
        grid=(b, nj),
        in_specs=([first_spec, second_spec] if two_in else [xspec])
        + [_const_spec((1, d)), _const_spec((1, d)), _const_spec(wgu.shape), _const_spec(wd.shape)],
        out_specs=out_specs,
        out_shape=out_shape,
        compiler_params=_cparams(("arbitrary", "arbitrary") if two_out else ("parallel", "parallel"), 56),
        name="ffn",
    )(*xs, npre.reshape(1, d), npost.reshape(1, d), wgu, wd)


INP_TM = 1024
INP_SUB = 512
_INP_OUTS = (("m2z", 256, BF16), ("small", 128, F32), ("gqk", 256, BF16), ("gv", 256, BF16), ("gg", 256, BF16),
             ("hq", 256, BF16), ("hf", 512, F32), ("hi", 256, BF16), ("hgg", 256, BF16))
N_HY = 3 * HY_C
N_XBC = BR_W + 4 * M2_N
N_INP = N_XBC + sum(w for _, w, _ in _INP_OUTS)


def _inproj_kernel(x_ref, xp_ref, xn_ref, nw_ref, wt_ref, wh_ref, cwt_ref, w_ref, cw_ref, cb_ref, hy_ref, xbc_ref,
                   *rest):
    i = pl.program_id(1)
    n = pl.num_programs(1)
    nw = nw_ref[...]
    tm = x_ref.shape[1]
    sub = min(INP_SUB, tm)
    hs = [_rms(x_ref[0, r * sub:(r + 1) * sub, :], nw).astype(BF16) for r in range(tm // sub)]
    hpn = _rms(jnp.concatenate([xp_ref[0], xn_ref[0]], axis=0), nw).astype(BF16)
    first, last = i == 0, i == n - 1

    p = jnp.concatenate([_dot_nt(wt_ref[...], h) for h in hs], axis=1)
    halo = _dot(hpn, wh_ref[...])
    halo_t = jnp.concatenate([halo, jnp.zeros((LANES - 2 * SUBLANES, N_HY), F32)], axis=0).T
    pp = jnp.where(first, 0.0, halo_t[:, SUBLANES - 1:SUBLANES])
    pn = jnp.where(last, 0.0, halo_t[:, SUBLANES:SUBLANES + 1])
    lane = lax.broadcasted_iota(jnp.int32, (1, tm), 1)
    prev = jnp.where(lane == 0, pp, pltpu.roll(p, 1, 1))
    nxt = jnp.where(lane == tm - 1, pn, pltpu.roll(p, tm - 1, 1))
    cwt = cwt_ref[...]
    hy_ref[0] = (cwt[:, 3:4] + prev * cwt[:, 0:1] + p * cwt[:, 1:2] + nxt * cwt[:, 2:3]).astype(hy_ref.dtype)

    wc = w_ref[:, :N_XBC]
    p = jnp.concatenate([_dot(h, wc) for h in hs], axis=0)
    halo = _dot(hpn, wc)
    pp = jnp.where(first, 0.0, halo[SUBLANES - 1:SUBLANES])
    pn = jnp.where(last, 0.0, halo[SUBLANES:SUBLANES + 1])
    row = lax.broadcasted_iota(jnp.int32, (tm, 1), 0)
    prev = jnp.where(row == 0, pp, pltpu.roll(p, 1, 0))
    nxt = jnp.where(row == tm - 1, pn, pltpu.roll(p, tm - 1, 0))
    cw = cw_ref[...]
    xbc_ref[0] = _silu(cb_ref[...] + prev * cw[0:1] + p * cw[1:2] + nxt * cw[2:3]).astype(xbc_ref.dtype)
    off = N_XBC
    for (_, width, _), ref in zip(_INP_OUTS, rest):
        for r, h in enumerate(hs):
            ref[0, r * sub:(r + 1) * sub, :] = _dot(h, w_ref[:, off:off + width]).astype(ref.dtype)
        off += width


def _inproj(x, nw, w_hy, conv_hyt, w_p, conv_w, conv_b):
    b, l, d = x.shape
    tm = min(INP_TM, l)
    r8 = tm // SUBLANES
    nb8 = l // SUBLANES
    outs = (("xbc", N_XBC, BF16),) + _INP_OUTS
    w_hyt = w_hy.T
    return pl.pallas_call(
        _inproj_kernel,
        grid=(b, l // tm),
        in_specs=[pl.BlockSpec((1, tm, d), lambda i, j: (i, j, 0)),
                  pl.BlockSpec((1, SUBLANES, d), lambda i, j: (i, jnp.maximum(j * r8 - 1, 0), 0)),
                  pl.BlockSpec((1, SUBLANES, d), lambda i, j: (i, jnp.minimum((j + 1) * r8, nb8 - 1), 0)),
                  _const_spec((1, d)), _const_spec(w_hyt.shape), _const_spec(w_hy.shape), _const_spec(conv_hyt.shape),
                  _const_spec(w_p.shape), _const_spec(conv_w.shape), _const_spec(conv_b.shape)],
        out_specs=[pl.BlockSpec((1, N_HY, tm), lambda i, j: (i, 0, j))]
        + [pl.BlockSpec((1, tm, w), lambda i, j: (i, j, 0)) for _, w, _ in outs],
        out_shape=[jax.ShapeDtypeStruct((b, N_HY, l), BF16)]
        + [jax.ShapeDtypeStruct((b, l, w), dt) for _, w, dt in outs],
        compiler_params=_cparams(("parallel", "parallel"), 48),
        name="inproj",
    )(x, x, x, nw.reshape(1, d), w_hyt, w_hy, conv_hyt, w_p, conv_w, conv_b)


MRG_TM = 1024
MRG_SUB = 512


def _merge_kernel(x_ref, ya_ref, yb_ref, yc_ref, yd_ref, npre_ref, npost_ref, wg_ref, wb_ref, wo_ref, o_ref):
    tm = x_ref.shape[1]
    sub = min(MRG_SUB, tm)
    for r in range(tm // sub):
        rs = slice(r * sub, (r + 1) * sub)
        x = x_ref[0, rs, :]
        h = _rms(x, npre_ref[...]).astype(BF16)
        merged = jnp.zeros(x.shape, F32)
        for n, y_ref in enumerate((ya_ref, yb_ref, yc_ref, yd_ref)):
            gate = jax.nn.sigmoid(_dot(h, wg_ref[:, n * D_MODEL:(n + 1) * D_MODEL]))
            if n == 0:
                proj = _dot_tn(y_ref[0, :, rs].astype(BF16), wb_ref[n])
            else:
                proj = _dot(y_ref[0, rs, :].astype(BF16), wb_ref[n])
            merged = merged + gate * proj
        out = _dot(merged.astype(BF16), wo_ref[...])
        o_ref[0, rs, :] = x + _rms(out, npost_ref[...])


def _merge(x, ys, npre, npost, w_gate, w_branch, w_out):
    b, l, d = x.shape
    tm = min(MRG_TM, l)
    xspec = pl.BlockSpec((1, tm, d), lambda i, j: (i, j, 0))
    yspec = pl.BlockSpec((1, tm, BR_W), lambda i, j: (i, j, 0))
    return pl.pallas_call(
        _merge_kernel,
        grid=(b, l // tm),
        in_specs=[xspec, pl.BlockSpec((1, BR_W, tm), lambda i, j: (i, 0, j)), yspec, yspec, yspec,
                  _const_spec((1, d)), _const_spec((1, d)),
                  _const_spec(w_gate.shape), _const_spec(w_branch.shape), _const_spec(w_out.shape)],
        out_specs=xspec,
        out_shape=jax.ShapeDtypeStruct(x.shape, F32),
        compiler_params=_cparams(("parallel", "parallel"), 48),
        name="merge",
    )(x, *ys, npre.reshape(1, d), npost.reshape(1, d), w_gate, w_branch, w_out)


def _split_dot_left(m_bf16, x, passes):
    acc = None
    r = x
    for p in range(passes):
        piece = r.astype(BF16)
        t = _dot(m_bf16, piece)
        acc = t if acc is None else acc + t
        if p + 1 < passes:
            r = r - piece.astype(F32)
    return acc


def _split_dot_right(x, m_bf16, passes):
    acc = None
    r = x
    for p in range(passes):
        piece = r.astype(BF16)
        t = _dot(piece, m_bf16)
        acc = t if acc is None else acc + t
        if p + 1 < passes:
            r = r - piece.astype(F32)
    return acc


def _log_sigmoid(x):
    return jnp.minimum(x, 0.0) - jnp.log1p(jnp.exp(-jnp.abs(x)))


def _softplus(x):
    return jnp.maximum(x, 0.0) + jnp.log1p(jnp.exp(-jnp.abs(x)))


def _ones_where(mask):
    return jnp.where(mask, 1.0, 0.0).astype(BF16)


def _head_norm(o, nw_row, dv):
    cv = o.shape[1]
    r = lax.broadcasted_iota(jnp.int32, (cv, cv), 0) // dv
    s = lax.broadcasted_iota(jnp.int32, (cv, cv), 1) // dv
    ms = _split_dot_right(o * o, _ones_where(r == s), 2) * (1.0 / dv)
    return o * lax.rsqrt(ms + EPS) * nw_row


SCAN_TB = 128
SCAN_NSEQ = {"gla": 6, "hgrn": 3, "ssd": 6}
GLA_DIAG = 8
NEG_BIG = -1e30


def _gla_levels(tb):
    out, m = [], GLA_DIAG
    while m < tb:
        out.append(m)
        m *= 2
    return out


def _gla_consts(tb, ck, dk, rev):
    c = GLA_DIAG
    nh = ck // dk
    t = np.arange(tb)
    tau = tb - 1 - t if rev else t
    tt, rr = tau[:, None], tau[None, :]
    col = np.arange(LANES)[None, :]
    ecat = col == (np.tile(np.arange(ck) // dk, c) * c + np.repeat(np.arange(c), ck))[:, None]
    spread = np.arange(LANES)[:, None] == (np.repeat(np.arange(nh), tb) * c + np.tile(tau % c, nh))[None, :]
    dmask = np.tile((tt // c == rr // c) & (rr <= tt), (1, nh))
    return jnp.asarray(rr <= tt, BF16), jnp.asarray(ecat, BF16), jnp.asarray(spread, BF16), jnp.asarray(dmask, F32)


def _interleave(gens):
    results = [None] * len(gens)
    active = list(enumerate(gens))
    while active:
        still = []
        for i, g in active:
            try:
                r = next(g)
            except StopIteration:
                continue
            if r is not None:
                results[i] = r
            still.append((i, g))
        active = still
    return results


def _gla_block(q, k, v, la, st, tril_ref, ecat_ref, spread_ref, dmask_ref, *, rev, dk, dv):
    tb, ck = q.shape
    cv = v.shape[1]
    c = GLA_DIAG
    nh = ck // dk
    levels = _gla_levels(tb)

    def rows(a, b):
        return slice(tb - b, tb - a) if rev else slice(a, b)

    def row(x):
        return rows(x, x + 1)

    cb = _split_dot_left(tril_ref[...], la, 3)
    total = cb[row(tb - 1)]
    yield
    lane_hk = lax.broadcasted_iota(jnp.int32, (1, ck), 1) // dk
    lane_hv = lax.broadcasted_iota(jnp.int32, (1, cv), 1) // dv
    vh = [jnp.where(lane_hv == h, v, 0.0) for h in range(nh)]

    qe = (q * jnp.exp(cb)).astype(BF16)
    ke = (k * jnp.exp(total - cb)).astype(BF16)
    o = _dot_nt(qe, st.astype(BF16))
    hmask = (lax.broadcasted_iota(jnp.int32, (cv, ck), 0) // dv) == (lax.broadcasted_iota(jnp.int32, (cv, ck), 1) // dk)
    st_new = st * jnp.exp(total) + jnp.where(hmask, _dot_tn(v.astype(BF16), ke), 0.0)
    yield

    g3 = (tb // c, c, ck)
    q3, k3, cb3 = q.reshape(g3), k.reshape(g3), (cb * LOG2E).reshape(g3)
    ps = []
    for j in range(c):
        jt = c - 1 - j if rev else j
        e = jnp.exp2(jnp.minimum(cb3 - jnp.broadcast_to(cb3[:, jt:jt + 1, :], g3), 0.0))
        ps.append((q3 * jnp.broadcast_to(k3[:, jt:jt + 1, :], g3) * e).reshape(tb, ck).astype(BF16))
        if j % 2 == 1:
            yield
    att8 = _dot(jnp.concatenate(ps, axis=1), ecat_ref[...]).astype(BF16)
    yield
    att = _dot(att8, spread_ref[...]) * dmask_ref[...]
    yield
    o = o + _dot(att.astype(BF16), jnp.concatenate(vh, axis=0).astype(BF16))
    yield

    tau = lax.broadcasted_iota(jnp.int32, (tb, 1), 0)
    if rev:
        tau = tb - 1 - tau
    scores = []
    for li, m in enumerate(levels):
        early = (tau % (2 * m)) < m
        npairs = tb // (2 * m)
        bnd = [None] * npairs
        for p in range(npairs):
            bnd[npairs - 1 - p if rev else p] = jnp.broadcast_to(cb[row(p * 2 * m + m - 1)], (2 * m, ck))
        diff = cb - jnp.concatenate(bnd, axis=0)
        w = jnp.exp(jnp.where(early, -diff, diff)) * jnp.where(early, k, q)
        yield
        npad = (-nh * m) % LANES
        for p in range(tb // (2 * m)):
            er, lr = rows(p * 2 * m, p * 2 * m + m), rows(p * 2 * m + m, (p + 1) * 2 * m)
            kst = [jnp.where(lane_hk == h, w[er], 0.0) for h in range(nh)]
            kst += [jnp.zeros((npad, ck), F32)] if npad else []
            a = _dot_nt(w[lr].astype(BF16), jnp.concatenate(kst, axis=0).astype(BF16))
            scores.append((m, p, er, a.astype(BF16)))
            yield
    for m in levels:
        npad = (-nh * m) % LANES
        parts = [None] * (tb // m)
        for mm, p, er, a in scores:
            if mm != m:
                continue
            vst = [vh[h][er] for h in range(nh)] + ([jnp.zeros((npad, cv), F32)] if npad else [])
            early_idx, late_idx = (2 * p, 2 * p + 1)
            if rev:
                early_idx, late_idx = tb // m - 1 - early_idx, tb // m - 1 - late_idx
            parts[late_idx] = _dot(a, jnp.concatenate(vst, axis=0).astype(BF16))
            parts[early_idx] = jnp.zeros((m, cv), F32)
            yield
        o = o + jnp.concatenate(parts, axis=0)
    yield o, st_new


def _reset_state(st_ref):
    @pl.when(pl.program_id(1) == 0)
    def _():
        st_ref[...] = jnp.zeros(st_ref.shape, F32)


N_SCAN_CONSTS = 4


def _gla_kernel(*refs, rev, final):
    qk_ref, v_ref, sm_ref = refs[:3]
    consts = refs[3:3 + N_SCAN_CONSTS]
    if final:
        wgh_ref, wgl_ref, bg_ref, oprev_ref, g_ref, nw_ref, o_ref, st_ref = refs[3 + N_SCAN_CONSTS:]
    else:
        wgh_ref, wgl_ref, bg_ref, o_ref, st_ref = refs[3 + N_SCAN_CONSTS:]
    _reset_state(st_ref)
    ck = GLA_H * GLA_DK
    nseq = st_ref.shape[0]
    steps = []
    for s in range(nseq):
        qk = qk_ref[s].astype(F32)
        q = qk[:, :ck] * (GLA_DK ** -0.5)
        k = qk[:, ck:]
        v = v_ref[s].astype(F32)
        a = sm_ref[s]
        a_hi = a.astype(BF16)
        a_lo = (a - a_hi.astype(F32)).astype(BF16)
        pre = _dot(a_hi, wgh_ref[...]) + _dot(a_lo, wgh_ref[...]) + _dot(a_hi, wgl_ref[...]) + bg_ref[...]
        la = _log_sigmoid(pre) * (1.0 / GLA_GATE_NORM)
        steps.append(_gla_block(q, k, v, la, st_ref[s], *consts, rev=rev, dk=GLA_DK, dv=GLA_DV))
    for s, (o, st_new) in enumerate(_interleave(steps)):
        st_ref[s] = st_new
        if final:
            o = o + oprev_ref[s]
            o = (_head_norm(o, nw_ref[...], GLA_DV) * _silu(g_ref[s].astype(F32))).astype(o_ref.dtype)
        o_ref[s] = o


def _hg_kernel(*refs, rev, final):
    q_ref, f_ref, i_ref = refs[:3]
    consts = refs[3:3 + N_SCAN_CONSTS]
    if final:
        lb_ref, oprev_ref, g_ref, nw_ref, o_ref, st_ref = refs[3 + N_SCAN_CONSTS:]
    else:
        lb_ref, o_ref, st_ref = refs[3 + N_SCAN_CONSTS:]
    _reset_state(st_ref)
    lb = lb_ref[...]
    steps = []
    for s in range(st_ref.shape[0]):
        q = q_ref[s].astype(F32) * (HG_DK ** -0.5)
        z = f_ref[s]
        pos = z >= 0.0
        e = jnp.exp(-jnp.abs(z))
        t = 1.0 + e
        r = 1.0 / t
        la = jnp.minimum(z, 0.0) - jnp.log(t) + jnp.log(1.0 + lb * jnp.where(pos, e, 1.0 / e))
        k = (1.0 - lb) * jnp.where(pos, e * r, r)
        v = i_ref[s].astype(F32)
        steps.append(_gla_block(q, k, v, la, st_ref[s], *consts, rev=rev, dk=HG_DK, dv=HG_DV))
    for s, (o, st_new) in enumerate(_interleave(steps)):
        st_ref[s] = st_new
        if final:
            o = o + oprev_ref[s]
            o = (_head_norm(o, nw_ref[...], HG_DV) * _silu(g_ref[s].astype(F32))).astype(o_ref.dtype)
        o_ref[s] = o


def _scan_call(body, name, seq_ins, const_ins, fin_seq, fin_const, scratch, b, l, rev):
    tb = min(SCAN_TB, l)
    nblk = l // tb
    final = fin_seq is not None
    nseq = math.gcd(b, SCAN_NSEQ[name.split("_")[0]])

    def seq_spec(width, lane_idx):
        if rev:
            return pl.BlockSpec((nseq, tb, width), lambda i, j: (i, nblk - 1 - j, lane_idx))
        return pl.BlockSpec((nseq, tb, width), lambda i, j: (i, j, lane_idx))

    arrays = [a for a, _, _ in seq_ins] + list(const_ins)
    specs = [seq_spec(w, li) for _, w, li in seq_ins] + [_const_spec(a.shape) for a in const_ins]
    if final:
        arrays += [a for a, _, _ in fin_seq] + list(fin_const)
        specs += [seq_spec(w, li) for _, w, li in fin_seq] + [_const_spec(a.shape) for a in fin_const]
    return pl.pallas_call(
        functools.partial(body, rev=rev, final=final),
        grid=(b // nseq, nblk),
        in_specs=specs,
        out_specs=seq_spec(BR_W, 0),
        out_shape=jax.ShapeDtypeStruct((b, l, BR_W), BF16 if final else F32),
        scratch_shapes=[pltpu.VMEM((nseq,) + tuple(scratch), F32)],
        compiler_params=_cparams(("parallel", "arbitrary"), 32),
        name=name,
    )(*arrays)


def _gla_branch(gqk, gv, small, gg, wg2, bg, nw):
    b, l, _ = gqk.shape
    tb = min(SCAN_TB, l)
    ck = GLA_H * GLA_DK
    seq = [(gqk, 256, 0), (gv, 256, 0), (small, 128, 0)]
    scratch = (BR_W, ck)
    wg_hi = wg2.astype(BF16)
    wg_lo = (wg2 - wg_hi.astype(F32)).astype(BF16)
    o_f = _scan_call(_gla_kernel, "gla_fwd", seq, [*_gla_consts(tb, ck, GLA_DK, False), wg_hi[0], wg_lo[0], bg[0]],
                     None, None, scratch, b, l, False)
    return _scan_call(_gla_kernel, "gla_bwd", seq, [*_gla_consts(tb, ck, GLA_DK, True), wg_hi[1], wg_lo[1], bg[1]],
                      [(o_f, 256, 0), (gg, 256, 0)], [nw], scratch, b, l, True)


def _hg_branch(hq, hf, hi, hgg, lb, nw):
    b, l, _ = hq.shape
    tb = min(SCAN_TB, l)
    ck = HG_H * HG_DK
    scratch = (BR_W, ck)
    o_f = _scan_call(_hg_kernel, "hgrn_fwd", [(hq, 256, 0), (hf, 256, 0), (hi, 256, 0)],
                     [*_gla_consts(tb, ck, HG_DK, False), lb[0]], None, None, scratch, b, l, False)
    return _scan_call(_hg_kernel, "hgrn_bwd", [(hq, 256, 0), (hf, 256, 1), (hi, 256, 0)],
                      [*_gla_consts(tb, ck, HG_DK, True), lb[1]],
                      [(o_f, 256, 0), (hgg, 256, 0)], [nw], scratch, b, l, True)


def _ssd_kernel(*refs, rev, final):
    if final:
        xbc_ref, sm_ref, dtb_ref, a_ref, oprev_ref, z_ref, dskip_ref, nw_ref, o_ref, st_ref = refs
    else:
        xbc_ref, sm_ref, dtb_ref, a_ref, o_ref, st_ref = refs
    _reset_state(st_ref)
    steps = [_ssd_block(xbc_ref[s].astype(F32), sm_ref[s], dtb_ref[...], a_ref[...], st_ref[s], rev)
             for s in range(st_ref.shape[0])]
    for s, (y, x, st_new) in enumerate(_interleave(steps)):
        st_ref[s] = st_new
        if final:
            y = y + oprev_ref[s] + dskip_ref[...] * x
            o_ref[s] = _rms(y * _silu(z_ref[s].astype(F32)), nw_ref[...]).astype(o_ref.dtype)
        else:
            o_ref[s] = y


def _ssd_block(xbc, sm, dtb, a_row, st, rev):
    tb = xbc.shape[0]
    x = xbc[:, :BR_W]
    bm = xbc[:, BR_W:BR_W + 2 * M2_N].astype(BF16)
    cm = xbc[:, BR_W + 2 * M2_N:]
    dt = _softplus(sm + dtb)
    da = dt * a_row
    lane0 = SM_DT + (M2_H if rev else 0)

    t_i = lax.broadcasted_iota(jnp.int32, (tb, tb), 0)
    s_i = lax.broadcasted_iota(jnp.int32, (tb, tb), 1)
    if rev:
        incl, strict = s_i >= t_i, t_i < s_i
    else:
        incl, strict = s_i <= t_i, t_i > s_i
    m_incl = _ones_where(incl)
    edge = 0 if rev else tb - 1
    cum = _split_dot_left(m_incl, da, 3)
    suf = cum[edge:edge + 1, :] - cum
    yield
    expand = _ones_where((lax.broadcasted_iota(jnp.int32, (LANES, BR_W), 0) - lane0)
                         == (lax.broadcasted_iota(jnp.int32, (LANES, BR_W), 1) // M2_P))
    dt_x = _split_dot_right(dt, expand, 2)
    yield
    ecum_x = _split_dot_right(jnp.exp(cum), expand, 2)
    yield
    w_x = _split_dot_right(dt * jnp.exp(suf), expand, 2)
    yield
    xdt = (x * dt_x).astype(BF16)
    xw = (x * w_x).astype(BF16)

    lane_g = lax.broadcasted_iota(jnp.int32, (1, 2 * M2_N), 1) // M2_N
    lane_h = lax.broadcasted_iota(jnp.int32, (1, BR_W), 1) // M2_P
    gram = [_dot_nt(jnp.where(lane_g == g, cm, 0.0).astype(BF16), bm) for g in range(2)]
    yield
    y = jnp.zeros((tb, BR_W), F32)
    cols = [jnp.where(strict, jnp.broadcast_to(da[:, lane0 + h:lane0 + h + 1], (tb, tb)), 0.0) for h in range(M2_H)]
    e_all = _split_dot_left(m_incl, jnp.concatenate(cols, axis=1), 3)
    yield
    for h in range(M2_H):
        seg = jnp.exp(jnp.where(incl, e_all[:, h * tb:(h + 1) * tb], NEG_BIG))
        scores = (gram[h // (M2_H // 2)] * seg).astype(BF16)
        y = y + jnp.where(lane_h == h, _dot(scores, xdt), 0.0)
        yield
    y = y + ecum_x * _dot(cm.astype(BF16), st.astype(BF16))
    yield
    gmask = ((lax.broadcasted_iota(jnp.int32, st.shape, 0) // M2_N)
             == (lax.broadcasted_iota(jnp.int32, st.shape, 1) // (2 * M2_P)))
    yield y, x, st * ecum_x[edge:edge + 1, :] + jnp.where(gmask, _dot_tn(bm, xw), 0.0)


def _ssd_branch(xbc, small, m2z, dtb_row, a_row, dskip_row, nw):
    b, l, _ = xbc.shape
    seq = [(xbc, 512, 0), (small, 128, 0)]
    shape = (2 * M2_N, BR_W)
    o_f = _scan_call(_ssd_kernel, "ssd_fwd", seq, [dtb_row, a_row], None, None, shape, b, l, False)
    return _scan_call(_ssd_kernel, "ssd_bwd", seq, [dtb_row, a_row], [(o_f, 256, 0), (m2z, 256, 0)],
                      [dskip_row, nw], shape, b, l, True)


FFT_N2 = 128
FFT_CC = 32
HYF_TL = 512


def _hi_lo(x):
    hi = x.astype(BF16)
    return hi, (x - hi.astype(F32)).astype(BF16)


def _hyfilt_kernel(z_ref, t_ref, w1h_ref, w1l_ref, b1_ref, w2h_ref, w2l_ref, b2_ref, fr_ref, w3h_ref, w3l_ref,
                   dec_ref, o_ref):
    def dot3(a, bh_ref, bl_ref, dot):
        ah, al = _hi_lo(a)
        return dot(ah, bh_ref[...]) + dot(al, bh_ref[...]) + dot(ah, bl_ref[...])

    h = jnp.sin(fr_ref[0:1, :] * (dot3(z_ref[...], w1h_ref, w1l_ref, _dot) + b1_ref[...]))
    h = jnp.sin(fr_ref[1:2, :] * (dot3(h, w2h_ref, w2l_ref, _dot) + b2_ref[...]))
    window = jnp.exp(-dec_ref[...] * t_ref[...])
    o_ref[...] = dot3(h, w3h_ref, w3l_ref, lambda a, b: _dot_nt(b, a)) * window


def _hy_filter(zfeat, trow, w1, b1, w2, b2, freq, w3t, dec_col):
    l = zfeat.shape[0]
    tl = min(HYF_TL, l)
    nout = w3t.shape[0]
    consts = (*_hi_lo(w1), b1, *_hi_lo(w2), b2, freq, *_hi_lo(w3t), dec_col)
    return pl.pallas_call(
        _hyfilt_kernel,
        grid=(l // tl,),
        in_specs=[pl.BlockSpec((tl, LANES), lambda i: (i, 0)), pl.BlockSpec((1, tl), lambda i: (0, i))]
        + [_const_spec(a.shape) for a in consts],
        out_specs=pl.BlockSpec((nout, tl), lambda i: (0, i)),
        out_shape=jax.ShapeDtypeStruct((nout, l), F32),
        compiler_params=_cparams(("parallel",), 32),
        name="hyena_filter",
    )(zfeat, trow, *consts)


def _fft_tables(n1):
    n2 = FFT_N2
    n = n1 * n2
    k1 = np.arange(n1, dtype=np.float64)
    ang1 = 2.0 * np.pi * np.outer(k1, k1) / n1
    f1 = np.concatenate([np.cos(ang1), -np.sin(ang1)], axis=0)
    c3 = np.concatenate([np.cos(ang1), -np.sin(ang1)], axis=1)
    k2 = np.arange(n2, dtype=np.float64)
    ang2 = 2.0 * np.pi * np.outer(k2, k2) / n2
    c2, s2 = np.cos(ang2), np.sin(ang2)
    f2 = np.block([[c2, -s2], [s2, c2]])
    f2i = np.block([[c2, s2], [-s2, c2]])
    angt = 2.0 * np.pi * np.outer(k1, k2) / n
    bf = tuple(jnp.asarray(a, F32).astype(BF16) for a in (f1, c3, f2, f2i))
    return bf + (jnp.asarray(np.cos(angt), F32), jnp.asarray(-np.sin(angt), F32))


def _fft_forward(x3, f1, f2, tw_re, tw_im):
    cc = x3.shape[0]
    n1 = f1.shape[0] // 2
    xb = x3.astype(BF16)
    rows = []
    for c in range(0, cc, 2):
        a = _dot(f1, jnp.concatenate([xb[c], xb[c + 1]], axis=1))
        for a_re, a_im in ((a[:n1, :FFT_N2], a[n1:, :FFT_N2]), (a[:n1, FFT_N2:], a[n1:, FFT_N2:])):
            rows.append(jnp.concatenate([a_re * tw_re - a_im * tw_im, a_re * tw_im + a_im * tw_re], axis=1))
    return _dot(jnp.concatenate(rows, axis=0).astype(BF16), f2)


def _spectrum_kernel(hf_ref, hb_ref, f1_ref, f2_ref, twre_ref, twim_ref, ore_ref, oim_ref):
    cc, nt1, n2 = hf_ref.shape
    n1 = 2 * nt1
    tabs = (f1_ref[...], f2_ref[...], twre_ref[...], twim_ref[...])
    origin = (lax.broadcasted_iota(jnp.int32, (nt1, n2), 0) == 0) & (lax.broadcasted_iota(jnp.int32, (nt1, n2), 1) == 0)
    xf = _fft_forward(hf_ref[...], *tabs)
    xb = _fft_forward(jnp.where(origin, 0.0, hb_ref[...]), *tabs)
    ore_ref[...] = (xf[:, :n2] + xb[:, :n2]).reshape(cc, n1, n2)
    oim_ref[...] = (xf[:, n2:] - xb[:, n2:]).reshape(cc, n1, n2)


def _conv_kernel(u_ref, gate_ref, bias_ref, hre_ref, him_ref, f1_ref, c3_ref, f2_ref, f2i_ref, twre_ref, twim_ref,
                 o_ref):
    _, cc, nt1, n2 = u_ref.shape
    n1 = 2 * nt1
    tw_re, tw_im = twre_ref[...], twim_ref[...]
    u = u_ref[0]
    x = _fft_forward(u, f1_ref[...], f2_ref[...], tw_re, tw_im)
    x_re, x_im = x[:, :n2], x[:, n2:]
    h_re, h_im = hre_ref[...].reshape(cc * n1, n2), him_ref[...].reshape(cc * n1, n2)
    y = jnp.concatenate([x_re * h_re - x_im * h_im, x_re * h_im + x_im * h_re], axis=1).astype(BF16)
    cm = _dot(y, f2i_ref[...])
    c3 = c3_ref[...]
    scale = 1.0 / (n1 * n2)
    for c in range(0, cc, 2):
        d = []
        for ch in (c, c + 1):
            c_re, c_im = cm[ch * n1:(ch + 1) * n1, :n2], cm[ch * n1:(ch + 1) * n1, n2:]
            d.append(jnp.concatenate([c_re * tw_re + c_im * tw_im, c_im * tw_re - c_re * tw_im], axis=0))
        yt = _dot(c3, jnp.concatenate(d, axis=1).astype(BF16))
        for k, ch in enumerate((c, c + 1)):
            conv_out = yt[:, k * n2:(k + 1) * n2] * scale + bias_ref[ch] * u[ch].astype(F32)
            o_ref[0, ch] = (gate_ref[0, ch].astype(F32) * conv_out).astype(o_ref.dtype)


def _fft_spectrum(filt4, tables):
    no, _, c, nt1, n2 = filt4.shape
    cc = FFT_CC
    ncb = c // cc
    f1, _, f2, _, tw_re, tw_im = tables
    f1 = f1[:, :nt1]

    def in_spec(direction):
        return pl.BlockSpec((None, None, cc, nt1, n2), lambda o, i: (o, direction, i, 0, 0))

    out_spec = pl.BlockSpec((cc, 2 * nt1, n2), lambda o, i: (o * ncb + i, 0, 0))
    return pl.pallas_call(
        _spectrum_kernel,
        grid=(no, ncb),
        in_specs=[in_spec(0), in_spec(1)] + [_const_spec(a.shape) for a in (f1, f2, tw_re, tw_im)],
        out_specs=[out_spec, out_spec],
        out_shape=[jax.ShapeDtypeStruct((no * c, 2 * nt1, n2), F32)] * 2,
        compiler_params=_cparams(("parallel", "parallel"), 48),
        name="hyena_spectrum",
    )(filt4, filt4, f1, f2, tw_re, tw_im)


def _fft_conv(u4, u_ch, gate4, gate_ch, bias3, hre, him, h_ch, tables):
    b, _, nt1, n2 = u4.shape
    n1 = 2 * nt1
    cc = FFT_CC
    f1, c3, f2, f2i, tw_re, tw_im = tables
    f1 = f1[:, :nt1]
    c3 = c3[:nt1]

    def seq_spec(ch0):
        return pl.BlockSpec((1, cc, nt1, n2), lambda ct, i: (i, ch0 // cc + ct, 0, 0))

    hspec = pl.BlockSpec((cc, n1, n2), lambda ct, i: (h_ch // cc + ct, 0, 0))
    return pl.pallas_call(
        _conv_kernel,
        grid=(BR_W // cc, b),
        in_specs=[seq_spec(u_ch), seq_spec(gate_ch), pl.BlockSpec((cc, 1, n2), lambda ct, i: (ct, 0, 0)), hspec, hspec]
        + [_const_spec(a.shape) for a in (f1, c3, f2, f2i, tw_re, tw_im)],
        out_specs=seq_spec(0),
        out_shape=jax.ShapeDtypeStruct((b, BR_W, nt1, n2), BF16),
        compiler_params=_cparams(("parallel", "parallel"), 48),
        name="hyena_conv",
    )(u4, gate4, bias3, hre, him, f1, c3, f2, f2i, tw_re, tw_im)


def _hyena_features(l):
    t = np.linspace(0.0, 1.0, l)[:, None]
    bands = np.linspace(1e-4, HY_BANDS - 1, HY_BANDS)[None, :]
    w = (2.0 * np.pi / l) * np.arange(l)[:, None]
    z = np.concatenate([t, np.cos(bands * w), -np.sin(bands * w)], axis=-1)
    return jnp.asarray(np.pad(z, ((0, 0), (0, LANES - HY_EMB))), F32), jnp.asarray(t.T, F32)


def _hyena_branch(hy, lp, tables):
    b, _, l = hy.shape
    n2 = FFT_N2
    nt1 = l // n2
    pad = LANES - HY_FH
    w1 = jnp.pad(lp['hy_filt_w1'], ((0, LANES - HY_EMB), (0, pad)))
    w2 = jnp.pad(lp['hy_filt_w2'], ((0, pad), (0, pad)))
    w3t = jnp.pad(lp['hy_filt_w3'], ((0, pad), (0, 0))).T
    b1 = jnp.pad(lp['hy_filt_b1'], (0, pad)).reshape(1, LANES)
    b2 = jnp.pad(lp['hy_filt_b2'], (0, pad)).reshape(1, LANES)
    freq = jnp.pad(lp['hy_filt_freq'], ((0, 0), (0, pad)))
    dec_col = jnp.abs(lp['hy_decay']).reshape(-1, 1)
    filt = _hy_filter(*_hyena_features(l), w1, b1, w2, b2, freq, w3t, dec_col)
    hre, him = _fft_spectrum(filt.reshape(2, 2, HY_C, nt1, n2), tables)
    hy4 = hy.reshape(b, 3 * HY_C, nt1, n2)
    bias3 = jnp.broadcast_to(lp['hy_bias'][:, :, None, None], (2, HY_C, 1, n2))
    z = _fft_conv(hy4, 0, hy4, HY_C, bias3[0], hre, him, 0, tables)
    y = _fft_conv(z, 0, hy4, 2 * HY_C, bias3[1], hre, him, HY_C, tables)
    return y.reshape(b, HY_C, l)


def _mixer(x, lp, tables):
    w = lp['w_in']
    zpad = jnp.zeros((D_MODEL, LANES - 2 * GLA_R - 2 * M2_H), w.dtype)
    w_p = jnp.concatenate(
        [w[:, O_XBC:O_DT], w[:, O_M2Z:O_XBC], w[:, O_GA:O_GG], w[:, O_DT:O_GQ], zpad,
         w[:, O_GQ:O_GV], w[:, O_GV:O_GA], w[:, O_GG:O_HQ], w[:, O_HQ:O_HF], w[:, O_HF:O_HI], w[:, O_HI:O_HG],
         w[:, O_HG:O_GATE]], axis=1).astype(BF16)
    conv_hyt = jnp.concatenate([lp['hy_conv_w'], lp['hy_conv_b'][None, :]], axis=0).T
    hy, xbc, m2z, small, gqk, gv, gg, hq, hf, hi, hgg = _inproj(
        x, lp['mix_norm_pre'], w[:, O_HY:O_M2Z].astype(BF16), conv_hyt, w_p, lp['m2_conv_w'],
        lp['m2_conv_b'].reshape(1, N_XBC))

    y_a = _hyena_branch(hy, lp, tables)

    dt_lanes = slice(SM_DT, SM_DT + 2 * M2_H)
    dtb_row = jnp.zeros((1, LANES), F32).at[0, dt_lanes].set(lp['m2_dt_bias'])
    a_row = jnp.zeros((1, LANES), F32).at[0, dt_lanes].set(-jnp.exp(lp['m2_A_log'].astype(F32)))
    dskip_row = jnp.repeat(lp['m2_D'], M2_P).reshape(1, BR_W)
    y_b = _ssd_branch(xbc, small, m2z, dtb_row, a_row, dskip_row, lp['m2_norm'].reshape(1, BR_W))

    wg2 = jnp.zeros((2, LANES, GLA_H * GLA_DK), F32)
    for dr in range(2):
        wg2 = wg2.at[dr, SM_GA + dr * GLA_R:SM_GA + (dr + 1) * GLA_R].set(lp['gla_w_gate2'][dr])
    y_c = _gla_branch(gqk, gv, small, gg, wg2, lp['gla_b_gate'].reshape(2, 1, GLA_H * GLA_DK),
                      jnp.tile(lp['gla_norm'], GLA_H).reshape(1, BR_W))

    y_d = _hg_branch(hq, hf, hi, hgg, lp['hg_lb'].reshape(2, 1, HG_H * HG_DK),
                     jnp.tile(lp['hg_norm'], HG_H).reshape(1, BR_W))

    return _merge(x, (y_a, y_b, y_c, y_d), lp['mix_norm_pre'], lp['mix_norm_post'],
                  w[:, O_GATE:].astype(BF16), lp['w_branch'].astype(BF16), lp['w_out'].astype(BF16))


def kernel(x_prompt, x_sample, ffn1_norm_pre, ffn1_norm_post, ffn1_w_gu, ffn1_w_down, mix_norm_pre, mix_norm_post, w_in, hy_conv_w, hy_conv_b, hy_filt_w1, hy_filt_b1, hy_filt_w2, hy_filt_b2, hy_filt_freq, hy_filt_w3, hy_decay, hy_bias, m2_conv_w, m2_conv_b, m2_dt_bias, m2_A_log, m2_D, m2_norm, gla_w_gate2, gla_b_gate, gla_norm, hg_lb_param, hg_norm, w_branch, w_out, ffn2_norm_pre, ffn2_norm_post, ffn2_w_gu, ffn2_w_down):
    sm = jax.nn.softmax(hg_lb_param.astype(F32), axis=0)
    hg_lb = jnp.cumsum(sm, axis=0) - sm[0]
    params = {
        'mix_norm_pre': mix_norm_pre, 'mix_norm_post': mix_norm_post, 'w_in': w_in,
        'hy_conv_w': hy_conv_w, 'hy_conv_b': hy_conv_b,
        'hy_filt_w1': hy_filt_w1, 'hy_filt_b1': hy_filt_b1, 'hy_filt_w2': hy_filt_w2, 'hy_filt_b2': hy_filt_b2,
        'hy_filt_freq': hy_filt_freq, 'hy_filt_w3': hy_filt_w3, 'hy_decay': hy_decay, 'hy_bias': hy_bias,
        'm2_conv_w': m2_conv_w, 'm2_conv_b': m2_conv_b, 'm2_dt_bias': m2_dt_bias, 'm2_A_log': m2_A_log,
        'm2_D': m2_D, 'm2_norm': m2_norm,
        'gla_w_gate2': gla_w_gate2, 'gla_b_gate': gla_b_gate, 'gla_norm': gla_norm,
        'hg_lb': hg_lb, 'hg_norm': hg_norm, 'w_branch': w_branch, 'w_out': w_out,
    }
    assert x_prompt.shape[1:] == x_sample.shape[1:]
    x = (x_prompt, x_sample)
    depth = w_in.shape[0]
    tables = _fft_tables(2 * x_prompt.shape[1] // FFT_N2)
    for layer in range(depth):
        lp = {name: arr[layer] for name, arr in params.items()}
        x = _ffn(x, ffn1_norm_pre[layer], ffn1_norm_post[layer],
                 ffn1_w_gu[layer].astype(BF16), ffn1_w_down[layer].astype(BF16))
        x = _mixer(x, lp, tables)
        x = _ffn(x, ffn2_norm_pre[layer], ffn2_norm_post[layer],
                 ffn2_w_gu[layer].astype(BF16), ffn2_w_down[layer].astype(BF16),
                 n_first=x_prompt.shape[0], two_out=layer == depth - 1)
    return tuple(x)
```

```python
import functools
import math

import numpy as np
import jax
import jax.numpy as jnp
from jax import lax
from jax.experimental import pallas as pl
from jax.experimental.pallas import tpu as pltpu

F32 = jnp.float32
BF16 = jnp.bfloat16
EPS = 1e-6
LOG2E = 1.4426950408889634

SUBLANES = 8
LANES = 128
VMEM_BYTES = 64 * 1024 * 1024

D_MODEL = 1024
BR_W = 256
D_FF = 2816
HY_C = 256
HY_BANDS = 16
HY_EMB = 33
HY_FH = 64
M2_H = 4
M2_P = 64
M2_N = 64
GLA_H = 4
GLA_DK = 32
GLA_DV = 64
GLA_R = 16
GLA_GATE_NORM = 16.0
HG_H = 4
HG_DK = 64
HG_DV = 64
N_GATE = 4 * D_MODEL
O_HY, O_M2Z, O_XBC, O_DT, O_GQ, O_GK, O_GV, O_GA, O_GG, O_HQ, O_HF, O_HI, O_HG, O_GATE = (
    0, 768, 1024, 1536, 1544, 1672, 1800, 2056, 2088, 2344, 2600, 3112, 3368, 3624)
N_CONV = 768 + 512
SM_GA = 0
SM_DT = 32


def _cparams(sem, vmem_mb):
    return pltpu.CompilerParams(dimension_semantics=sem, vmem_limit_bytes=vmem_mb * 1024 * 1024)


def _dot(a, b):
    return jnp.dot(a, b, preferred_element_type=F32)


def _dot_nt(a, b):
    return lax.dot_general(a, b, (((1,), (1,)), ((), ())), preferred_element_type=F32)


def _dot_tn(a, b):
    return lax.dot_general(a, b, (((0,), (0,)), ((), ())), preferred_element_type=F32)


def _rms(x, w):
    return x * lax.rsqrt(jnp.mean(x * x, axis=-1, keepdims=True) + EPS) * w


def _silu(x):
    return x * jax.nn.sigmoid(x)


def _const_spec(shape):
    nd = len(shape)
    return pl.BlockSpec(shape, lambda *_: (0,) * nd, pipeline_mode=pl.Buffered(1))


FFN_TM = 1024
FFN_SUB = 512
FFN_FC = 256


def _ffn_kernel(*refs, n_first, two_in, two_out):
    refs = list(refs)
    xa_ref = refs.pop(0)
    xb_ref = refs.pop(0) if two_in else None
    npre_ref, npost_ref, wgu_ref, wd_ref = refs[:4]
    outs = refs[4:]
    in_first = pl.program_id(0) < n_first if (two_in or two_out) else None
    x_tile = jnp.where(in_first, xa_ref[0], xb_ref[0]) if two_in else xa_ref[0]
    pieces = []
    sub = min(FFN_SUB, x_tile.shape[0])
    for r in range(x_tile.shape[0] // sub):
        x = x_tile[r * sub:(r + 1) * sub]
        xn = _rms(x, npre_ref[...]).astype(BF16)
        acc = jnp.zeros(x.shape, F32)
        for j in range(D_FF // FFN_FC):
            g = _dot(xn, wgu_ref[:, j * FFN_FC:(j + 1) * FFN_FC])
            u = _dot(xn, wgu_ref[:, D_FF + j * FFN_FC:D_FF + (j + 1) * FFN_FC])
            h = (_silu(g) * u).astype(BF16)
            acc = acc + _dot(h, wd_ref[j * FFN_FC:(j + 1) * FFN_FC, :])
        pieces.append(x + 0.5 * _rms(acc, npost_ref[...]))
    out = jnp.concatenate(pieces, axis=0)
    if two_out:
        @pl.when(in_first)
        def _():
            outs[0][0] = out

        @pl.when(jnp.logical_not(in_first))
        def _():
            outs[1][0] = out
    else:
        outs[0][0] = out


def _ffn(xs, npre, npost, wgu, wd, n_first=None, two_out=False):
    two_in = isinstance(xs, tuple)
    if two_in:
        n_first = xs[0].shape[0]
        b = n_first + xs[1].shape[0]
    else:
        xs = (xs,)
        b = xs[0].shape[0]
    _, l, d = xs[0].shape
    tm = min(FFN_TM, l)
    nj = l // tm
    xspec = pl.BlockSpec((1, tm, d), lambda i, j: (i, j, 0))
    first_spec = pl.BlockSpec((1, tm, d), lambda i, j: (jnp.minimum(i, n_first - 1), jnp.where(i < n_first, j, nj - 1), 0))
    second_spec = pl.BlockSpec((1, tm, d), lambda i, j: (jnp.maximum(i - n_first, 0), jnp.where(i < n_first, 0, j), 0))
    if two_out:
        out_specs = [first_spec, second_spec]
        out_shape = [jax.ShapeDtypeStruct((n_first, l, d), F32), jax.ShapeDtypeStruct((b - n_first, l, d), F32)]
    else:
        out_specs, out_shape = xspec, jax.ShapeDtypeStruct((b, l, d), F32)
    return pl.pallas_call(
        functools.partial(_ffn_kernel, n_first=n_first, two_in=two_in, two_out=two_out),
        grid=(b, nj),
        in_specs=([first_spec, second_spec] if two_in else [xspec])
        + [_const_spec((1, d)), _const_spec((1, d)), _const_spec(wgu.shape), _const_spec(wd.shape)],
        out_specs=out_specs,
        out_shape=out_shape,
        compiler_params=_cparams(("arbitrary", "arbitrary") if two_out else ("parallel", "parallel"), 56),
        name="ffn",
    )(*xs, npre.reshape(1, d), npost.reshape(1, d), wgu, wd)


INP_TM = 1024
INP_SUB = 512
_INP_OUTS = (("m2z", 256, BF16), ("small", 128, F32), ("gqk", 256, BF16), ("gv", 256, BF16), ("gg", 256, BF16),
             ("hq", 256, BF16), ("hf", 512, F32), ("hi", 256, BF16), ("hgg", 256, BF16))
N_HY = 3 * HY_C
N_XBC = BR_W + 4 * M2_N
N_INP = N_XBC + sum(w for _, w, _ in _INP_OUTS)


def _inproj_kernel(x_ref, xp_ref, xn_ref, nw_ref, wt_ref, wh_ref, cwt_ref, w_ref, cw_ref, cb_ref, hy_ref, xbc_ref,
                   *rest):
    i = pl.program_id(1)
    n = pl.num_programs(1)
    nw = nw_ref[...]
    tm = x_ref.shape[1]
    sub = min(INP_SUB, tm)
    hs = [_rms(x_ref[0, r * sub:(r + 1) * sub, :], nw).astype(BF16) for r in range(tm // sub)]
    hpn = _rms(jnp.concatenate([xp_ref[0], xn_ref[0]], axis=0), nw).astype(BF16)
    first, last = i == 0, i == n - 1

    p = jnp.concatenate([_dot_nt(wt_ref[...], h) for h in hs], axis=1)
    halo = _dot(hpn, wh_ref[...])
    halo_t = jnp.concatenate([halo, jnp.zeros((LANES - 2 * SUBLANES, N_HY), F32)], axis=0).T
    pp = jnp.where(first, 0.0, halo_t[:, SUBLANES - 1:SUBLANES])
    pn = jnp.where(last, 0.0, halo_t[:, SUBLANES:SUBLANES + 1])
    lane = lax.broadcasted_iota(jnp.int32, (1, tm), 1)
    prev = jnp.where(lane == 0, pp, pltpu.roll(p, 1, 1))
    nxt = jnp.where(lane == tm - 1, pn, pltpu.roll(p, tm - 1, 1))
    cwt = cwt_ref[...]
    u = cwt[:, 3:4] + prev * cwt[:, 0:1] + p * cwt[:, 1:2] + nxt * cwt[:, 2:3]
    hy_ref[0] = u.reshape(N_HY, tm // LANES, LANES).astype(hy_ref.dtype)

    wc = w_ref[:, :N_XBC]
    p = jnp.concatenate([_dot(h, wc) for h in hs], axis=0)
    halo = _dot(hpn, wc)
    pp = jnp.where(first, 0.0, halo[SUBLANES - 1:SUBLANES])
    pn = jnp.where(last, 0.0, halo[SUBLANES:SUBLANES + 1])
    row = lax.broadcasted_iota(jnp.int32, (tm, 1), 0)
    prev = jnp.where(row == 0, pp, pltpu.roll(p, 1, 0))
    nxt = jnp.where(row == tm - 1, pn, pltpu.roll(p, tm - 1, 0))
    cw = cw_ref[...]
    xbc_ref[0] = _silu(cb_ref[...] + prev * cw[0:1] + p * cw[1:2] + nxt * cw[2:3]).astype(xbc_ref.dtype)
    off = N_XBC
    for (_, width, _), ref in zip(_INP_OUTS, rest):
        for r, h in enumerate(hs):
            ref[0, r * sub:(r + 1) * sub, :] = _dot(h, w_ref[:, off:off + width]).astype(ref.dtype)
        off += width


def _inproj(x, nw, w_hy, conv_hyt, w_p, conv_w, conv_b):
    b, l, d = x.shape
    tm = min(INP_TM, l)
    r8 = tm // SUBLANES
    nb8 = l // SUBLANES
    outs = (("xbc", N_XBC, BF16),) + _INP_OUTS
    w_hyt = w_hy.T
    return pl.pallas_call(
        _inproj_kernel,
        grid=(b, l // tm),
        in_specs=[pl.BlockSpec((1, tm, d), lambda i, j: (i, j, 0)),
                  pl.BlockSpec((1, SUBLANES, d), lambda i, j: (i, jnp.maximum(j * r8 - 1, 0), 0)),
                  pl.BlockSpec((1, SUBLANES, d), lambda i, j: (i, jnp.minimum((j + 1) * r8, nb8 - 1), 0)),
                  _const_spec((1, d)), _const_spec(w_hyt.shape), _const_spec(w_hy.shape), _const_spec(conv_hyt.shape),
                  _const_spec(w_p.shape), _const_spec(conv_w.shape), _const_spec(conv_b.shape)],
        out_specs=[pl.BlockSpec((1, N_HY, tm // LANES, LANES), lambda i, j: (i, 0, j, 0))]
        + [pl.BlockSpec((1, tm, w), lambda i, j: (i, j, 0)) for _, w, _ in outs],
        out_shape=[jax.ShapeDtypeStruct((b, N_HY, l // LANES, LANES), BF16)]
        + [jax.ShapeDtypeStruct((b, l, w), dt) for _, w, dt in outs],
        compiler_params=_cparams(("parallel", "parallel"), 48),
        name="inproj",
    )(x, x, x, nw.reshape(1, d), w_hyt, w_hy, conv_hyt, w_p, conv_w, conv_b)


MRG_TM = 1024
MRG_SUB = 512


def _merge_kernel(x_ref, ya_ref, yb_ref, yc_ref, yd_ref, npre_ref, npost_ref, wg_ref, wb_ref, wo_ref, o_ref):
    tm = x_ref.shape[1]
    sub = min(MRG_SUB, tm)
    for r in range(tm // sub):
        rs = slice(r * sub, (r + 1) * sub)
        x = x_ref[0, rs, :]
        h = _rms(x, npre_ref[...]).astype(BF16)
        merged = jnp.zeros(x.shape, F32)
        for n, y_ref in enumerate((ya_ref, yb_ref, yc_ref, yd_ref)):
            gate = jax.nn.sigmoid(_dot(h, wg_ref[:, n * D_MODEL:(n + 1) * D_MODEL]))
            if n == 0:
                proj = _dot_tn(y_ref[0, :, rs].astype(BF16), wb_ref[n])
            else:
                proj = _dot(y_ref[0, rs, :].astype(BF16), wb_ref[n])
            merged = merged + gate * proj
        out = _dot(merged.astype(BF16), wo_ref[...])
        o_ref[0, rs, :] = x + _rms(out, npost_ref[...])


def _merge(x, ys, npre, npost, w_gate, w_branch, w_out):
    b, l, d = x.shape
    tm = min(MRG_TM, l)
    xspec = pl.BlockSpec((1, tm, d), lambda i, j: (i, j, 0))
    yspec = pl.BlockSpec((1, tm, BR_W), lambda i, j: (i, j, 0))
    return pl.pallas_call(
        _merge_kernel,
        grid=(b, l // tm),
        in_specs=[xspec, pl.BlockSpec((1, BR_W, tm), lambda i, j: (i, 0, j)), yspec, yspec, yspec,
                  _const_spec((1, d)), _const_spec((1, d)),
                  _const_spec(w_gate.shape), _const_spec(w_branch.shape), _const_spec(w_out.shape)],
        out_specs=xspec,
        out_shape=jax.ShapeDtypeStruct(x.shape, F32),
        compiler_params=_cparams(("parallel", "parallel"), 48),
        name="merge",
    )(x, *ys, npre.reshape(1, d), npost.reshape(1, d), w_gate, w_branch, w_out)


def _split_dot_left(m_bf16, x, passes):
    acc = None
    r = x
    for p in range(passes):
        piece = r.astype(BF16)
        t = _dot(m_bf16, piece)
        acc = t if acc is None else acc + t
        if p + 1 < passes:
            r = r - piece.astype(F32)
    return acc


def _split_dot_right(x, m_bf16, passes):
    acc = None
    r = x
    for p in range(passes):
        piece = r.astype(BF16)
        t = _dot(piece, m_bf16)
        acc = t if acc is None else acc + t
        if p + 1 < passes:
            r = r - piece.astype(F32)
    return acc


def _log_sigmoid(x):
    return jnp.minimum(x, 0.0) - jnp.log1p(jnp.exp(-jnp.abs(x)))


def _softplus(x):
    return jnp.maximum(x, 0.0) + jnp.log1p(jnp.exp(-jnp.abs(x)))


def _ones_where(mask):
    return jnp.where(mask, 1.0, 0.0).astype(BF16)


def _head_norm(o, nw_row, dv):
    cv = o.shape[1]
    r = lax.broadcasted_iota(jnp.int32, (cv, cv), 0) // dv
    s = lax.broadcasted_iota(jnp.int32, (cv, cv), 1) // dv
    ms = _split_dot_right(o * o, _ones_where(r == s), 2) * (1.0 / dv)
    return o * lax.rsqrt(ms + EPS) * nw_row


SCAN_TB = 128
SCAN_NSEQ = {"gla": 6, "hgrn": 6, "ssd": 6}
GLA_DIAG = 8
NEG_BIG = -1e30


def _gla_levels(tb):
    out, m = [], GLA_DIAG
    while m < tb:
        out.append(m)
        m *= 2
    return out


def _gla_consts(tb, ck, dk, rev):
    c = GLA_DIAG
    nh = ck // dk
    t = np.arange(tb)
    tau = tb - 1 - t if rev else t
    tt, rr = tau[:, None], tau[None, :]
    col = np.arange(LANES)[None, :]
    ecat = col == (np.tile(np.arange(ck) // dk, c) * c + np.repeat(np.arange(c), ck))[:, None]
    spread = np.arange(LANES)[:, None] == (np.repeat(np.arange(nh), tb) * c + np.tile(tau % c, nh))[None, :]
    dmask = np.tile((tt // c == rr // c) & (rr <= tt), (1, nh))
    return jnp.asarray(rr <= tt, BF16), jnp.asarray(ecat, BF16), jnp.asarray(spread, BF16), jnp.asarray(dmask, F32)


def _interleave(gens):
    results = [None] * len(gens)
    active = list(enumerate(gens))
    while active:
        still = []
        for i, g in active:
            try:
                r = next(g)
            except StopIteration:
                continue
            if r is not None:
                results[i] = r
            still.append((i, g))
        active = still
    return results


def _gla_block(q, k, v, la, st, tril_ref, ecat_ref, spread_ref, dmask_ref, *, rev, dk, dv):
    tb, ck = q.shape
    cv = v.shape[1]
    c = GLA_DIAG
    nh = ck // dk
    levels = _gla_levels(tb)

    def rows(a, b):
        return slice(tb - b, tb - a) if rev else slice(a, b)

    def row(x):
        return rows(x, x + 1)

    cb = _split_dot_left(tril_ref[...], la, 3)
    total = cb[row(tb - 1)]
    yield
    lane_hk = lax.broadcasted_iota(jnp.int32, (1, ck), 1) // dk
    lane_hv = lax.broadcasted_iota(jnp.int32, (1, cv), 1) // dv
    vh = [jnp.where(lane_hv == h, v, 0.0) for h in range(nh)]

    qe = (q * jnp.exp(cb)).astype(BF16)
    ke = (k * jnp.exp(total - cb)).astype(BF16)
    o = _dot_nt(qe, st.astype(BF16))
    hmask = (lax.broadcasted_iota(jnp.int32, (cv, ck), 0) // dv) == (lax.broadcasted_iota(jnp.int32, (cv, ck), 1) // dk)
    st_new = st * jnp.exp(total) + jnp.where(hmask, _dot_tn(v.astype(BF16), ke), 0.0)
    yield

    g3 = (tb // c, c, ck)
    q3, k3, cb3 = q.reshape(g3), k.reshape(g3), (cb * LOG2E).reshape(g3)
    ps = []
    for j in range(c):
        jt = c - 1 - j if rev else j
        e = jnp.exp2(jnp.minimum(cb3 - jnp.broadcast_to(cb3[:, jt:jt + 1, :], g3), 0.0))
        ps.append((q3 * jnp.broadcast_to(k3[:, jt:jt + 1, :], g3) * e).reshape(tb, ck).astype(BF16))
        if j % 2 == 1:
            yield
    att8 = _dot(jnp.concatenate(ps, axis=1), ecat_ref[...]).astype(BF16)
    yield
    att = _dot(att8, spread_ref[...]) * dmask_ref[...]
    yield
    o = o + _dot(att.astype(BF16), jnp.concatenate(vh, axis=0).astype(BF16))
    yield

    tau = lax.broadcasted_iota(jnp.int32, (tb, 1), 0)
    if rev:
        tau = tb - 1 - tau
    scores = []
    for li, m in enumerate(levels):
        early = (tau % (2 * m)) < m
        npairs = tb // (2 * m)
        bnd = [None] * npairs
        for p in range(npairs):
            bnd[npairs - 1 - p if rev else p] = jnp.broadcast_to(cb[row(p * 2 * m + m - 1)], (2 * m, ck))
        diff = cb - jnp.concatenate(bnd, axis=0)
        w = jnp.exp(jnp.where(early, -diff, diff)) * jnp.where(early, k, q)
        yield
        npad = (-nh * m) % LANES
        for p in range(tb // (2 * m)):
            er, lr = rows(p * 2 * m, p * 2 * m + m), rows(p * 2 * m + m, (p + 1) * 2 * m)
            kst = [jnp.where(lane_hk == h, w[er], 0.0) for h in range(nh)]
            kst += [jnp.zeros((npad, ck), F32)] if npad else []
            a = _dot_nt(w[lr].astype(BF16), jnp.concatenate(kst, axis=0).astype(BF16))
            scores.append((m, p, er, a.astype(BF16)))
            yield
    for m in levels:
        npad = (-nh * m) % LANES
        parts = [None] * (tb // m)
        for mm, p, er, a in scores:
            if mm != m:
                continue
            vst = [vh[h][er] for h in range(nh)] + ([jnp.zeros((npad, cv), F32)] if npad else [])
            early_idx, late_idx = (2 * p, 2 * p + 1)
            if rev:
                early_idx, late_idx = tb // m - 1 - early_idx, tb // m - 1 - late_idx
            parts[late_idx] = _dot(a, jnp.concatenate(vst, axis=0).astype(BF16))
            parts[early_idx] = jnp.zeros((m, cv), F32)
            yield
        o = o + jnp.concatenate(parts, axis=0)
    yield o, st_new


def _reset_state(st_ref):
    @pl.when(pl.program_id(1) == 0)
    def _():
        st_ref[...] = jnp.zeros(st_ref.shape, F32)


N_SCAN_CONSTS = 4


def _gla_kernel(*refs, rev, final):
    qk_ref, v_ref, sm_ref = refs[:3]
    consts = refs[3:3 + N_SCAN_CONSTS]
    if final:
        wgh_ref, wgl_ref, bg_ref, oprev_ref, g_ref, nw_ref, o_ref, st_ref = refs[3 + N_SCAN_CONSTS:]
    else:
        wgh_ref, wgl_ref, bg_ref, o_ref, st_ref = refs[3 + N_SCAN_CONSTS:]
    _reset_state(st_ref)
    ck = GLA_H * GLA_DK
    nseq = st_ref.shape[0]
    steps = []
    for s in range(nseq):
        qk = qk_ref[s].astype(F32)
        q = qk[:, :ck] * (GLA_DK ** -0.5)
        k = qk[:, ck:]
        v = v_ref[s].astype(F32)
        a = sm_ref[s]
        a_hi = a.astype(BF16)
        a_lo = (a - a_hi.astype(F32)).astype(BF16)
        pre = _dot(a_hi, wgh_ref[...]) + _dot(a_lo, wgh_ref[...]) + _dot(a_hi, wgl_ref[...]) + bg_ref[...]
        la = _log_sigmoid(pre) * (1.0 / GLA_GATE_NORM)
        steps.append(_gla_block(q, k, v, la, st_ref[s], *consts, rev=rev, dk=GLA_DK, dv=GLA_DV))
    for s, (o, st_new) in enumerate(_interleave(steps)):
        st_ref[s] = st_new
        if final:
            o = o + oprev_ref[s]
            o = (_head_norm(o, nw_ref[...], GLA_DV) * _silu(g_ref[s].astype(F32))).astype(o_ref.dtype)
        o_ref[s] = o


def _hg_kernel(*refs, rev, final):
    q_ref, f_ref, i_ref = refs[:3]
    consts = refs[3:3 + N_SCAN_CONSTS]
    if final:
        lb_ref, oprev_ref, g_ref, nw_ref, o_ref, st_ref = refs[3 + N_SCAN_CONSTS:]
    else:
        lb_ref, o_ref, st_ref = refs[3 + N_SCAN_CONSTS:]
    _reset_state(st_ref)
    lb = lb_ref[...]
    steps = []
    for s in range(st_ref.shape[0]):
        q = q_ref[s].astype(F32) * (HG_DK ** -0.5)
        z = f_ref[s]
        pos = z >= 0.0
        e = jnp.exp(-jnp.abs(z))
        t = 1.0 + e
        r = 1.0 / t
        la = jnp.minimum(z, 0.0) - jnp.log(t) + jnp.log(1.0 + lb * jnp.where(pos, e, 1.0 / e))
        k = (1.0 - lb) * jnp.where(pos, e * r, r)
        v = i_ref[s].astype(F32)
        steps.append(_gla_block(q, k, v, la, st_ref[s], *consts, rev=rev, dk=HG_DK, dv=HG_DV))
    for s, (o, st_new) in enumerate(_interleave(steps)):
        st_ref[s] = st_new
        if final:
            o = o + oprev_ref[s]
            o = (_head_norm(o, nw_ref[...], HG_DV) * _silu(g_ref[s].astype(F32))).astype(o_ref.dtype)
        o_ref[s] = o


def _scan_call(body, name, seq_ins, const_ins, fin_seq, fin_const, scratch, b, l, rev):
    tb = min(SCAN_TB, l)
    nblk = l // tb
    final = fin_seq is not None
    nseq = math.gcd(b, SCAN_NSEQ[name.split("_")[0]])

    def seq_spec(width, lane_idx):
        if rev:
            return pl.BlockSpec((nseq, tb, width), lambda i, j: (i, nblk - 1 - j, lane_idx))
        return pl.BlockSpec((nseq, tb, width), lambda i, j: (i, j, lane_idx))

    arrays = [a for a, _, _ in seq_ins] + list(const_ins)
    specs = [seq_spec(w, li) for _, w, li in seq_ins] + [_const_spec(a.shape) for a in const_ins]
    if final:
        arrays += [a for a, _, _ in fin_seq] + list(fin_const)
        specs += [seq_spec(w, li) for _, w, li in fin_seq] + [_const_spec(a.shape) for a in fin_const]
    return pl.pallas_call(
        functools.partial(body, rev=rev, final=final),
        grid=(b // nseq, nblk),
        in_specs=specs,
        out_specs=seq_spec(BR_W, 0),
        out_shape=jax.ShapeDtypeStruct((b, l, BR_W), BF16 if final else F32),
        scratch_shapes=[pltpu.VMEM((nseq,) + tuple(scratch), F32)],
        compiler_params=_cparams(("parallel", "arbitrary"), 32),
        name=name,
    )(*arrays)


def _gla_branch(gqk, gv, small, gg, wg2, bg, nw):
    b, l, _ = gqk.shape
    tb = min(SCAN_TB, l)
    ck = GLA_H * GLA_DK
    seq = [(gqk, 256, 0), (gv, 256, 0), (small, 128, 0)]
    scratch = (BR_W, ck)
    wg_hi = wg2.astype(BF16)
    wg_lo = (wg2 - wg_hi.astype(F32)).astype(BF16)
    o_f = _scan_call(_gla_kernel, "gla_fwd", seq, [*_gla_consts(tb, ck, GLA_DK, False), wg_hi[0], wg_lo[0], bg[0]],
                     None, None, scratch, b, l, False)
    return _scan_call(_gla_kernel, "gla_bwd", seq, [*_gla_consts(tb, ck, GLA_DK, True), wg_hi[1], wg_lo[1], bg[1]],
                      [(o_f, 256, 0), (gg, 256, 0)], [nw], scratch, b, l, True)


def _hg_branch(hq, hf, hi, hgg, lb, nw):
    b, l, _ = hq.shape
    tb = min(SCAN_TB, l)
    ck = HG_H * HG_DK
    scratch = (BR_W, ck)
    o_f = _scan_call(_hg_kernel, "hgrn_fwd", [(hq, 256, 0), (hf, 256, 0), (hi, 256, 0)],
                     [*_gla_consts(tb, ck, HG_DK, False), lb[0]], None, None, scratch, b, l, False)
    return _scan_call(_hg_kernel, "hgrn_bwd", [(hq, 256, 0), (hf, 256, 1), (hi, 256, 0)],
                      [*_gla_consts(tb, ck, HG_DK, True), lb[1]],
                      [(o_f, 256, 0), (hgg, 256, 0)], [nw], scratch, b, l, True)


def _ssd_kernel(*refs, rev, final):
    if final:
        xbc_ref, sm_ref, dtb_ref, a_ref, oprev_ref, z_ref, dskip_ref, nw_ref, o_ref, st_ref = refs
    else:
        xbc_ref, sm_ref, dtb_ref, a_ref, o_ref, st_ref = refs
    _reset_state(st_ref)
    steps = [_ssd_block(xbc_ref[s].astype(F32), sm_ref[s], dtb_ref[...], a_ref[...], st_ref[s], rev)
             for s in range(st_ref.shape[0])]
    for s, (y, x, st_new) in enumerate(_interleave(steps)):
        st_ref[s] = st_new
        if final:
            y = y + oprev_ref[s] + dskip_ref[...] * x
            o_ref[s] = _rms(y * _silu(z_ref[s].astype(F32)), nw_ref[...]).astype(o_ref.dtype)
        else:
            o_ref[s] = y


def _ssd_block(xbc, sm, dtb, a_row, st, rev):
    tb = xbc.shape[0]
    x = xbc[:, :BR_W]
    bm = xbc[:, BR_W:BR_W + 2 * M2_N].astype(BF16)
    cm = xbc[:, BR_W + 2 * M2_N:]
    dt = _softplus(sm + dtb)
    da = dt * a_row
    lane0 = SM_DT + (M2_H if rev else 0)

    t_i = lax.broadcasted_iota(jnp.int32, (tb, tb), 0)
    s_i = lax.broadcasted_iota(jnp.int32, (tb, tb), 1)
    if rev:
        incl, strict = s_i >= t_i, t_i < s_i
    else:
        incl, strict = s_i <= t_i, t_i > s_i
    m_incl = _ones_where(incl)
    edge = 0 if rev else tb - 1
    cum = _split_dot_left(m_incl, da, 3)
    suf = cum[edge:edge + 1, :] - cum
    yield
    expand = _ones_where((lax.broadcasted_iota(jnp.int32, (LANES, BR_W), 0) - lane0)
                         == (lax.broadcasted_iota(jnp.int32, (LANES, BR_W), 1) // M2_P))
    dt_x = _split_dot_right(dt, expand, 2)
    yield
    ecum_x = _split_dot_right(jnp.exp(cum), expand, 2)
    yield
    w_x = _split_dot_right(dt * jnp.exp(suf), expand, 2)
    yield
    xdt = (x * dt_x).astype(BF16)
    xw = (x * w_x).astype(BF16)

    lane_g = lax.broadcasted_iota(jnp.int32, (1, 2 * M2_N), 1) // M2_N
    lane_h = lax.broadcasted_iota(jnp.int32, (1, BR_W), 1) // M2_P
    gram = [_dot_nt(jnp.where(lane_g == g, cm, 0.0).astype(BF16), bm) for g in range(2)]
    yield
    y = jnp.zeros((tb, BR_W), F32)
    cols = [jnp.where(strict, jnp.broadcast_to(da[:, lane0 + h:lane0 + h + 1], (tb, tb)), 0.0) for h in range(M2_H)]
    e_all = _split_dot_left(m_incl, jnp.concatenate(cols, axis=1), 3)
    yield
    for h in range(M2_H):
        seg = jnp.exp(jnp.where(incl, e_all[:, h * tb:(h + 1) * tb], NEG_BIG))
        scores = (gram[h // (M2_H // 2)] * seg).astype(BF16)
        y = y + jnp.where(lane_h == h, _dot(scores, xdt), 0.0)
        yield
    y = y + ecum_x * _dot(cm.astype(BF16), st.astype(BF16))
    yield
    gmask = ((lax.broadcasted_iota(jnp.int32, st.shape, 0) // M2_N)
             == (lax.broadcasted_iota(jnp.int32, st.shape, 1) // (2 * M2_P)))
    yield y, x, st * ecum_x[edge:edge + 1, :] + jnp.where(gmask, _dot_tn(bm, xw), 0.0)


def _ssd_branch(xbc, small, m2z, dtb_row, a_row, dskip_row, nw):
    b, l, _ = xbc.shape
    seq = [(xbc, 512, 0), (small, 128, 0)]
    shape = (2 * M2_N, BR_W)
    o_f = _scan_call(_ssd_kernel, "ssd_fwd", seq, [dtb_row, a_row], None, None, shape, b, l, False)
    return _scan_call(_ssd_kernel, "ssd_bwd", seq, [dtb_row, a_row], [(o_f, 256, 0), (m2z, 256, 0)],
                      [dskip_row, nw], shape, b, l, True)


FFT_N2 = 128
FFT_CC = 32
HYF_TL = 512


def _hi_lo(x):
    hi = x.astype(BF16)
    return hi, (x - hi.astype(F32)).astype(BF16)


def _hyfilt_kernel(z_ref, t_ref, w1h_ref, w1l_ref, b1_ref, w2h_ref, w2l_ref, b2_ref, fr_ref, w3h_ref, w3l_ref,
                   dec_ref, o_ref):
    def dot3(a, bh_ref, bl_ref, dot):
        ah, al = _hi_lo(a)
        return dot(ah, bh_ref[...]) + dot(al, bh_ref[...]) + dot(ah, bl_ref[...])

    h = jnp.sin(fr_ref[0:1, :] * (dot3(z_ref[...], w1h_ref, w1l_ref, _dot) + b1_ref[...]))
    h = jnp.sin(fr_ref[1:2, :] * (dot3(h, w2h_ref, w2l_ref, _dot) + b2_ref[...]))
    window = jnp.exp(-dec_ref[...] * t_ref[...])
    o_ref[...] = dot3(h, w3h_ref, w3l_ref, lambda a, b: _dot_nt(b, a)) * window


def _hy_filter(zfeat, trow, w1, b1, w2, b2, freq, w3t, dec_col):
    l = zfeat.shape[0]
    tl = min(HYF_TL, l)
    nout = w3t.shape[0]
    consts = (*_hi_lo(w1), b1, *_hi_lo(w2), b2, freq, *_hi_lo(w3t), dec_col)
    return pl.pallas_call(
        _hyfilt_kernel,
        grid=(l // tl,),
        in_specs=[pl.BlockSpec((tl, LANES), lambda i: (i, 0)), pl.BlockSpec((1, tl), lambda i: (0, i))]
        + [_const_spec(a.shape) for a in consts],
        out_specs=pl.BlockSpec((nout, tl), lambda i: (0, i)),
        out_shape=jax.ShapeDtypeStruct((nout, l), F32),
        compiler_params=_cparams(("parallel",), 32),
        name="hyena_filter",
    )(zfeat, trow, *consts)


def _fft_tables(n1):
    n2 = FFT_N2
    n = n1 * n2
    k1 = np.arange(n1, dtype=np.float64)
    ang1 = 2.0 * np.pi * np.outer(k1, k1) / n1
    f1 = np.concatenate([np.cos(ang1), -np.sin(ang1)], axis=0)
    c3 = np.concatenate([np.cos(ang1), -np.sin(ang1)], axis=1)
    k2 = np.arange(n2, dtype=np.float64)
    ang2 = 2.0 * np.pi * np.outer(k2, k2) / n2
    c2, s2 = np.cos(ang2), np.sin(ang2)
    f2 = np.block([[c2, -s2], [s2, c2]])
    f2i = np.block([[c2, s2], [-s2, c2]])
    angt = 2.0 * np.pi * np.outer(k1, k2) / n
    bf = tuple(jnp.asarray(a, F32).astype(BF16) for a in (f1, c3, f2, f2i))
    return bf + (jnp.asarray(np.cos(angt), F32), jnp.asarray(-np.sin(angt), F32))


def _fft_forward(x3, f1, f2, tw_re, tw_im):
    cc = x3.shape[0]
    n1 = f1.shape[0] // 2
    xb = x3.astype(BF16)
    rows = []
    for c in range(0, cc, 2):
        a = _dot(f1, jnp.concatenate([xb[c], xb[c + 1]], axis=1))
        for a_re, a_im in ((a[:n1, :FFT_N2], a[n1:, :FFT_N2]), (a[:n1, FFT_N2:], a[n1:, FFT_N2:])):
            rows.append(jnp.concatenate([a_re * tw_re - a_im * tw_im, a_re * tw_im + a_im * tw_re], axis=1))
    return _dot(jnp.concatenate(rows, axis=0).astype(BF16), f2)


def _spectrum_kernel(hf_ref, hb_ref, f1_ref, f2_ref, twre_ref, twim_ref, ore_ref, oim_ref):
    cc, nt1, n2 = hf_ref.shape
    n1 = 2 * nt1
    tabs = (f1_ref[...], f2_ref[...], twre_ref[...], twim_ref[...])
    origin = (lax.broadcasted_iota(jnp.int32, (nt1, n2), 0) == 0) & (lax.broadcasted_iota(jnp.int32, (nt1, n2), 1) == 0)
    xf = _fft_forward(hf_ref[...], *tabs)
    xb = _fft_forward(jnp.where(origin, 0.0, hb_ref[...]), *tabs)
    ore_ref[...] = (xf[:, :n2] + xb[:, :n2]).reshape(cc, n1, n2)
    oim_ref[...] = (xf[:, n2:] - xb[:, n2:]).reshape(cc, n1, n2)


def _conv_kernel(u_ref, gate_ref, bias_ref, hre_ref, him_ref, f1_ref, c3_ref, f2_ref, f2i_ref, twre_ref, twim_ref,
                 o_ref):
    _, cc, nt1, n2 = u_ref.shape
    n1 = 2 * nt1
    tw_re, tw_im = twre_ref[...], twim_ref[...]
    u = u_ref[0]
    x = _fft_forward(u, f1_ref[...], f2_ref[...], tw_re, tw_im)
    x_re, x_im = x[:, :n2], x[:, n2:]
    h_re, h_im = hre_ref[...].reshape(cc * n1, n2), him_ref[...].reshape(cc * n1, n2)
    y = jnp.concatenate([x_re * h_re - x_im * h_im, x_re * h_im + x_im * h_re], axis=1).astype(BF16)
    cm = _dot(y, f2i_ref[...])
    c3 = c3_ref[...]
    scale = 1.0 / (n1 * n2)
    for c in range(0, cc, 2):
        d = []
        for ch in (c, c + 1):
            c_re, c_im = cm[ch * n1:(ch + 1) * n1, :n2], cm[ch * n1:(ch + 1) * n1, n2:]
            d.append(jnp.concatenate([c_re * tw_re + c_im * tw_im, c_im * tw_re - c_re * tw_im], axis=0))
        yt = _dot(c3, jnp.concatenate(d, axis=1).astype(BF16))
        for k, ch in enumerate((c, c + 1)):
            conv_out = yt[:, k * n2:(k + 1) * n2] * scale + bias_ref[ch] * u[ch].astype(F32)
            o_ref[0, ch] = (gate_ref[0, ch].astype(F32) * conv_out).astype(o_ref.dtype)


def _fft_spectrum(filt4, tables):
    no, _, c, nt1, n2 = filt4.shape
    cc = FFT_CC
    ncb = c // cc
    f1, _, f2, _, tw_re, tw_im = tables
    f1 = f1[:, :nt1]

    def in_spec(direction):
        return pl.BlockSpec((None, None, cc, nt1, n2), lambda o, i: (o, direction, i, 0, 0))

    out_spec = pl.BlockSpec((cc, 2 * nt1, n2), lambda o, i: (o * ncb + i, 0, 0))
    return pl.pallas_call(
        _spectrum_kernel,
        grid=(no, ncb),
        in_specs=[in_spec(0), in_spec(1)] + [_const_spec(a.shape) for a in (f1, f2, tw_re, tw_im)],
        out_specs=[out_spec, out_spec],
        out_shape=[jax.ShapeDtypeStruct((no * c, 2 * nt1, n2), F32)] * 2,
        compiler_params=_cparams(("parallel", "parallel"), 48),
        name="hyena_spectrum",
    )(filt4, filt4, f1, f2, tw_re, tw_im)


def _fft_conv(u4, u_ch, gate4, gate_ch, bias3, hre, him, h_ch, tables):
    b, _, nt1, n2 = u4.shape
    n1 = 2 * nt1
    cc = FFT_CC
    f1, c3, f2, f2i, tw_re, tw_im = tables
    f1 = f1[:, :nt1]
    c3 = c3[:nt1]

    def seq_spec(ch0):
        return pl.BlockSpec((1, cc, nt1, n2), lambda ct, i: (i, ch0 // cc + ct, 0, 0))

    hspec = pl.BlockSpec((cc, n1, n2), lambda ct, i: (h_ch // cc + ct, 0, 0))
    return pl.pallas_call(
        _conv_kernel,
        grid=(BR_W // cc, b),
        in_specs=[seq_spec(u_ch), seq_spec(gate_ch), pl.BlockSpec((cc, 1, n2), lambda ct, i: (ct, 0, 0)), hspec, hspec]
        + [_const_spec(a.shape) for a in (f1, c3, f2, f2i, tw_re, tw_im)],
        out_specs=seq_spec(0),
        out_shape=jax.ShapeDtypeStruct((b, BR_W, nt1, n2), BF16),
        compiler_params=_cparams(("parallel", "parallel"), 48),
        name="hyena_conv",
    )(u4, gate4, bias3, hre, him, f1, c3, f2, f2i, tw_re, tw_im)


def _hyena_features(l):
    t = np.linspace(0.0, 1.0, l)[:, None]
    bands = np.linspace(1e-4, HY_BANDS - 1, HY_BANDS)[None, :]
    w = (2.0 * np.pi / l) * np.arange(l)[:, None]
    z = np.concatenate([t, np.cos(bands * w), -np.sin(bands * w)], axis=-1)
    return jnp.asarray(np.pad(z, ((0, 0), (0, LANES - HY_EMB))), F32), jnp.asarray(t.T, F32)


def _hyena_branch(hy4, lp, tables):
    b, _, nt1, n2 = hy4.shape
    l = nt1 * n2
    pad = LANES - HY_FH
    w1 = jnp.pad(lp['hy_filt_w1'], ((0, LANES - HY_EMB), (0, pad)))
    w2 = jnp.pad(lp['hy_filt_w2'], ((0, pad), (0, pad)))
    w3t = jnp.pad(lp['hy_filt_w3'], ((0, pad), (0, 0))).T
    b1 = jnp.pad(lp['hy_filt_b1'], (0, pad)).reshape(1, LANES)
    b2 = jnp.pad(lp['hy_filt_b2'], (0, pad)).reshape(1, LANES)
    freq = jnp.pad(lp['hy_filt_freq'], ((0, 0), (0, pad)))
    dec_col = jnp.abs(lp['hy_decay']).reshape(-1, 1)
    filt = _hy_filter(*_hyena_features(l), w1, b1, w2, b2, freq, w3t, dec_col)
    hre, him = _fft_spectrum(filt.reshape(2, 2, HY_C, nt1, n2), tables)
    bias3 = jnp.broadcast_to(lp['hy_bias'][:, :, None, None], (2, HY_C, 1, n2))
    z = _fft_conv(hy4, 0, hy4, HY_C, bias3[0], hre, him, 0, tables)
    y = _fft_conv(z, 0, hy4, 2 * HY_C, bias3[1], hre, him, HY_C, tables)
    return y.reshape(b, HY_C, l)


def _mixer(x, lp, tables):
    w = lp['w_in']
    zpad = jnp.zeros((D_MODEL, LANES - 2 * GLA_R - 2 * M2_H), w.dtype)
    w_p = jnp.concatenate(
        [w[:, O_XBC:O_DT], w[:, O_M2Z:O_XBC], w[:, O_GA:O_GG], w[:, O_DT:O_GQ], zpad,
         w[:, O_GQ:O_GV], w[:, O_GV:O_GA], w[:, O_GG:O_HQ], w[:, O_HQ:O_HF], w[:, O_HF:O_HI], w[:, O_HI:O_HG],
         w[:, O_HG:O_GATE]], axis=1).astype(BF16)
    conv_hyt = jnp.concatenate([lp['hy_conv_w'], lp['hy_conv_b'][None, :]], axis=0).T
    hy, xbc, m2z, small, gqk, gv, gg, hq, hf, hi, hgg = _inproj(
        x, lp['mix_norm_pre'], w[:, O_HY:O_M2Z].astype(BF16), conv_hyt, w_p, lp['m2_conv_w'],
        lp['m2_conv_b'].reshape(1, N_XBC))

    y_a = _hyena_branch(hy, lp, tables)

    dt_lanes = slice(SM_DT, SM_DT + 2 * M2_H)
    dtb_row = jnp.zeros((1, LANES), F32).at[0, dt_lanes].set(lp['m2_dt_bias'])
    a_row = jnp.zeros((1, LANES), F32).at[0, dt_lanes].set(-jnp.exp(lp['m2_A_log'].astype(F32)))
    dskip_row = jnp.repeat(lp['m2_D'], M2_P).reshape(1, BR_W)
    y_b = _ssd_branch(xbc, small, m2z, dtb_row, a_row, dskip_row, lp['m2_norm'].reshape(1, BR_W))

    wg2 = jnp.zeros((2, LANES, GLA_H * GLA_DK), F32)
    for dr in range(2):
        wg2 = wg2.at[dr, SM_GA + dr * GLA_R:SM_GA + (dr + 1) * GLA_R].set(lp['gla_w_gate2'][dr])
    y_c = _gla_branch(gqk, gv, small, gg, wg2, lp['gla_b_gate'].reshape(2, 1, GLA_H * GLA_DK),
                      jnp.tile(lp['gla_norm'], GLA_H).reshape(1, BR_W))

    y_d = _hg_branch(hq, hf, hi, hgg, lp['hg_lb'].reshape(2, 1, HG_H * HG_DK),
                     jnp.tile(lp['hg_norm'], HG_H).reshape(1, BR_W))

    return _merge(x, (y_a, y_b, y_c, y_d), lp['mix_norm_pre'], lp['mix_norm_post'],
                  w[:, O_GATE:].astype(BF16), lp['w_branch'].astype(BF16), lp['w_out'].astype(BF16))


def kernel(x_prompt, x_sample, ffn1_norm_pre, ffn1_norm_post, ffn1_w_gu, ffn1_w_down, mix_norm_pre, mix_norm_post, w_in, hy_conv_w, hy_conv_b, hy_filt_w1, hy_filt_b1, hy_filt_w2, hy_filt_b2, hy_filt_freq, hy_filt_w3, hy_decay, hy_bias, m2_conv_w, m2_conv_b, m2_dt_bias, m2_A_log, m2_D, m2_norm, gla_w_gate2, gla_b_gate, gla_norm, hg_lb_param, hg_norm, w_branch, w_out, ffn2_norm_pre, ffn2_norm_post, ffn2_w_gu, ffn2_w_down):
    sm = jax.nn.softmax(hg_lb_param.astype(F32), axis=0)
    hg_lb = jnp.cumsum(sm, axis=0) - sm[0]
    params = {
        'mix_norm_pre': mix_norm_pre, 'mix_norm_post': mix_norm_post, 'w_in': w_in,
        'hy_conv_w': hy_conv_w, 'hy_conv_b': hy_conv_b,
        'hy_filt_w1': hy_filt_w1, 'hy_filt_b1': hy_filt_b1, 'hy_filt_w2': hy_filt_w2, 'hy_filt_b2': hy_filt_b2,
        'hy_filt_freq': hy_filt_freq, 'hy_filt_w3': hy_filt_w3, 'hy_decay': hy_decay, 'hy_bias': hy_bias,
        'm2_conv_w': m2_conv_w, 'm2_conv_b': m2_conv_b, 'm2_dt_bias': m2_dt_bias, 'm2_A_log': m2_A_log,
        'm2_D': m2_D, 'm2_norm': m2_norm,
        'gla_w_gate2': gla_w_gate2, 'gla_b_gate': gla_b_gate, 'gla_norm': gla_norm,
        'hg_lb': hg_lb, 'hg_norm': hg_norm, 'w_branch': w_branch, 'w_out': w_out,
    }
    assert x_prompt.shape[1:] == x_sample.shape[1:]
    x = (x_prompt, x_sample)
    depth = w_in.shape[0]
    tables = _fft_tables(2 * x_prompt.shape[1] // FFT_N2)
    for layer in range(depth):
        lp = {name: arr[layer] for name, arr in params.items()}
        x = _ffn(x, ffn1_norm_pre[layer], ffn1_norm_post[layer],
                 ffn1_w_gu[layer].astype(BF16), ffn1_w_down[layer].astype(BF16))
        x = _mixer(x, lp, tables)
        x = _ffn(x, ffn2_norm_pre[layer], ffn2_norm_post[layer],
                 ffn2_w_gu[layer].astype(BF16), ffn2_w_down[layer].astype(BF16),
                 n_first=x_prompt.shape[0], two_out=layer == depth - 1)
    return tuple(x)
```
